```python
import jax, jax.numpy as jnp
from jax import lax
import numpy as np

D_MODEL = 1024
BATCH = 8
SEQ = 4096
DEPTH = 2

MLA_HEADS = 8
QK_NOPE_DIM = 64
QK_ROPE_DIM = 32
QK_HEAD_DIM = QK_NOPE_DIM + QK_ROPE_DIM
V_HEAD_DIM = 64
Q_LORA_RANK = 256
KV_LORA_RANK = 128
ROPE_THETA = 10000.0
Q_BLOCK = 128
MLA_OUT = MLA_HEADS * V_HEAD_DIM
MLA_IN = Q_LORA_RANK + KV_LORA_RANK + QK_ROPE_DIM

RWKV_HEADS = 8
RWKV_HEAD_DIM = 64
RWKV_DIM = RWKV_HEADS * RWKV_HEAD_DIM
DECAY_LORA = 64
ICLR_LORA = 64
VRES_LORA = 32
GATE_LORA = 128
RWKV_GN_EPS = 64e-5
RWKV_IN = 3 * RWKV_DIM + DECAY_LORA + ICLR_LORA + GATE_LORA

N_IN = MLA_IN + RWKV_IN
MIX_DIM = MLA_OUT + RWKV_DIM

N_EXPERTS = 32
TOP_K = 4
D_FF_EXPERT = D_MODEL
SWIGLU_LIMIT = 7.0
SWIGLU_ALPHA = 1.702
EXPERT_BLOCK = 128

DEEPNORM_ALPHA = (2 * DEPTH) ** 0.25
DEEPNORM_BETA = (8 * DEPTH) ** -0.25
LN_EPS = 1e-5
RMS_EPS = 1e-6

kernel_name = 'hybrid_mla_rwkv7_moe_deepnorm_adaln'


def layer_norm(x, g, b):
    xf = x.astype(jnp.float32)
    mu = xf.mean(-1, keepdims=True)
    var = jnp.square(xf - mu).mean(-1, keepdims=True)
    return ((xf - mu) * lax.rsqrt(var + LN_EPS) * g + b).astype(x.dtype)


def rms_norm(x, g):
    xf = x.astype(jnp.float32)
    return (xf * lax.rsqrt(jnp.square(xf).mean(-1, keepdims=True) + RMS_EPS) * g).astype(x.dtype)


def rope(x, cos, sin):
    xp = x.astype(jnp.float32).reshape(x.shape[:-1] + (x.shape[-1] // 2, 2))
    x0, x1 = xp[..., 0], xp[..., 1]
    out = jnp.stack([x0 * cos - x1 * sin, x0 * sin + x1 * cos], axis=-1)
    return out.reshape(x.shape).astype(x.dtype)


def causal_attention(q, k, v):
    b, s, h, dq = q.shape
    nb = s // Q_BLOCK
    scale = QK_HEAD_DIM ** -0.5
    q_blocks = jnp.moveaxis(q.reshape(b, nb, Q_BLOCK, h, dq), 1, 0)
    k_pos = jnp.arange(s)

    def one_block(args):
        qb, bi = args
        scores = jnp.einsum('bqhd,bkhd->bhqk', qb, k, preferred_element_type=jnp.float32) * scale
        q_pos = bi * Q_BLOCK + jnp.arange(Q_BLOCK)
        scores = jnp.where(k_pos[None, :] <= q_pos[:, None], scores, -jnp.inf)
        probs = jax.nn.softmax(scores, axis=-1).astype(v.dtype)
        return jnp.einsum('bhqk,bkhd->bqhd', probs, v)

    out = lax.map(one_block, (q_blocks, jnp.arange(nb)))
    return jnp.moveaxis(out, 0, 1).reshape(b, s, h, v.shape[-1])


def mla_group(p, cos, sin, q_norm_g, w_uq, kv_norm_g, w_uk, w_uv, out_g):
    b, s, _ = p.shape
    q_lat = p[..., :Q_LORA_RANK]
    kv_lat = p[..., Q_LORA_RANK:Q_LORA_RANK + KV_LORA_RANK]
    k_rot = p[..., Q_LORA_RANK + KV_LORA_RANK:]
    q = (rms_norm(q_lat, q_norm_g) @ w_uq).reshape(b, s, MLA_HEADS, QK_HEAD_DIM)
    q = jnp.concatenate([q[..., :QK_NOPE_DIM],
                         rope(q[..., QK_NOPE_DIM:], cos[:, :, None, :], sin[:, :, None, :])], axis=-1)
    c_kv = rms_norm(kv_lat, kv_norm_g)
    k_nope = (c_kv @ w_uk).reshape(b, s, MLA_HEADS, QK_NOPE_DIM)
    v = (c_kv @ w_uv).reshape(b, s, MLA_HEADS, V_HEAD_DIM)
    k_rot = rope(k_rot, cos, sin)
    k = jnp.concatenate([k_nope, jnp.broadcast_to(k_rot[:, :, None, :], (b, s, MLA_HEADS, QK_ROPE_DIM))], axis=-1)
    o = causal_attention(q, k, v)
    return rms_norm(o.reshape(b, s, MLA_OUT), out_g)


def rwkv7_recurrence(r, decay, k, v, kk, a):
    b, s, h, n = r.shape

    def step(state, inp):
        r_t, w_t, k_t, v_t, kk_t, a_t = inp
        sa = jnp.einsum('bhij,bhj->bhi', state, -kk_t)
        state = (state * w_t[:, :, None, :]
                 + sa[..., :, None] * (kk_t * a_t)[..., None, :]
                 + v_t[..., :, None] * k_t[..., None, :])
        return state, jnp.einsum('bhij,bhj->bhi', state, r_t)

    xs = tuple(jnp.moveaxis(z, 1, 0) for z in (r, decay, k, v, kk, a))
    _, out = lax.scan(step, jnp.zeros((b, h, n, n), jnp.float32), xs)
    return jnp.moveaxis(out, 0, 1)


def rwkv7_group(p, mu, w0, w2, a0, a2, g2, k_k, k_a, r_k, gn_g, gn_b, v_first, vres):
    b, s, _ = p.shape
    h_, n_ = RWKV_HEADS, RWKV_HEAD_DIM
    p_prev = jnp.pad(p[:, :-1], ((0, 0), (1, 0), (0, 0)))
    p = p + mu * (p_prev - p)
    c3 = 3 * RWKV_DIM
    r = p[..., :RWKV_DIM]
    k = p[..., RWKV_DIM:2 * RWKV_DIM]
    v = p[..., 2 * RWKV_DIM:c3]
    wd = p[..., c3:c3 + DECAY_LORA]
    ad = p[..., c3 + DECAY_LORA:c3 + DECAY_LORA + ICLR_LORA]
    gd = p[..., c3 + DECAY_LORA + ICLR_LORA:]
    w = -jax.nn.softplus(-(w0 + jnp.tanh(wd) @ w2).astype(jnp.float32)) - 0.5
    decay = jnp.exp(-jnp.exp(w))
    a = jax.nn.sigmoid(a0 + ad @ a2)
    g = jax.nn.sigmoid(gd) @ g2
    if vres is not None:
        v0, v1, v2 = vres
        v = v + (v_first - v) * jax.nn.sigmoid(v0 + (v @ v1) @ v2)
    heads = lambda z: z.reshape(b, s, h_, n_).astype(jnp.float32)
    r_h, k_h, v_h, a_h, d_h = heads(r), heads(k), heads(v), heads(a), heads(decay)
    kk = k_h * k_k.reshape(h_, n_)
    kk = kk / jnp.maximum(jnp.linalg.norm(kk, axis=-1, keepdims=True), 1e-12)
    k_h = k_h * (1.0 + (a_h - 1.0) * k_a.reshape(h_, n_))
    o = rwkv7_recurrence(r_h, d_h, k_h, v_h, kk, a_h)
    mean = o.mean(-1, keepdims=True)
    var = jnp.square(o - mean).mean(-1, keepdims=True)
    o = ((o - mean) * lax.rsqrt(var + RWKV_GN_EPS)).reshape(b, s, RWKV_DIM) * gn_g + gn_b
    bonus = jnp.sum(r_h * k_h * r_k, axis=-1, keepdims=True) * v_h
    o = (o + bonus.reshape(b, s, RWKV_DIM)) * g
    return o.astype(p.dtype), v


def clamped_swiglu(hh):
    x_glu = jnp.minimum(hh[..., ::2], SWIGLU_LIMIT)
    x_lin = jnp.clip(hh[..., 1::2], -SWIGLU_LIMIT, SWIGLU_LIMIT)
    return x_glu * jax.nn.sigmoid(SWIGLU_ALPHA * x_glu) * (x_lin + 1.0)


def moe_ffn(h, router_w, router_b, w1, b1, w2, b2):
    b, s, d = h.shape
    t = b * s
    ht = h.reshape(t, d)
    logits = (ht @ router_w + router_b).astype(jnp.float32)
    top_val, top_idx = lax.top_k(logits, TOP_K)
    gates = jax.nn.softmax(top_val, axis=-1)
    n_assign = t * TOP_K
    e_flat = top_idx.reshape(n_assign)
    tok_flat = jnp.arange(n_assign, dtype=jnp.int32) // TOP_K
    order = jnp.argsort(e_flat)
    e_sorted = e_flat[order]
    counts = jnp.bincount(e_flat, length=N_EXPERTS)
    starts = jnp.cumsum(counts) - counts
    padded = (counts + EXPERT_BLOCK - 1) // EXPERT_BLOCK * EXPERT_BLOCK
    pad_ends = jnp.cumsum(padded)
    pad_starts = pad_ends - padded
    dest = pad_starts[e_sorted] + (jnp.arange(n_assign) - starts[e_sorted])
    n_blocks = -(-n_assign // EXPERT_BLOCK) + N_EXPERTS
    n_slots = n_blocks * EXPERT_BLOCK
    slot_tok = jnp.full((n_slots,), t, jnp.int32).at[dest].set(tok_flat[order])
    slot_gate = jnp.zeros((n_slots,), jnp.float32).at[dest].set(gates.reshape(n_assign)[order])
    block_expert = jnp.minimum(
        jnp.searchsorted(pad_ends, jnp.arange(n_blocks) * EXPERT_BLOCK, side='right'), N_EXPERTS - 1)
    h_pad = jnp.concatenate([ht, jnp.zeros((1, d), ht.dtype)], axis=0)

    def expert_block(args):
        idx, e = args
        u = clamped_swiglu(h_pad[idx] @ w1[e] + b1[e])
        return u @ w2[e] + b2[e]

    y_slots = lax.map(expert_block, (slot_tok.reshape(n_blocks, EXPERT_BLOCK), block_expert))
    y = jnp.zeros((t + 1, d), jnp.float32).at[slot_tok].add(
        y_slots.reshape(n_slots, d).astype(jnp.float32) * slot_gate[:, None])
    return y[:t].reshape(b, s, d).astype(h.dtype)


def setup_inputs(seed: int = 0) -> dict:
    key = jax.random.key(seed)
    keys = iter(jax.random.split(key, 64))

    def nrm(shape, scale):
        return jax.random.normal(next(keys), shape, jnp.float32) * scale

    L, D, C, E, F = DEPTH, D_MODEL, RWKV_DIM, N_EXPERTS, D_FF_EXPERT
    fan = D ** -0.5
    x = nrm((BATCH, SEQ, D), 1.0)
    c = nrm((BATCH, D), 1.0)
    positions = (jnp.arange(SEQ, dtype=jnp.int32)[None, :]
                 + jax.random.randint(next(keys), (BATCH, 1), 0, 2048, dtype=jnp.int32))
    emb_ln_g = 1.0 + nrm((D,), 0.02)
    emb_ln_b = nrm((D,), 0.02)
    ada_w = nrm((L, D, 6 * D), 0.1 * fan)
    ada_b = nrm((L, 6 * D), 0.02)
    w_in = jnp.concatenate([
        nrm((L, D, MLA_IN), fan),
        nrm((L, D, 2 * C), fan),
        nrm((L, D, C), fan * DEEPNORM_BETA),
        nrm((L, D, DECAY_LORA + ICLR_LORA + GATE_LORA), fan)], axis=-1)
    q_norm_g = 1.0 + nrm((L, Q_LORA_RANK), 0.02)
    w_uq = nrm((L, Q_LORA_RANK, MLA_HEADS * QK_HEAD_DIM), Q_LORA_RANK ** -0.5)
    kv_norm_g = 1.0 + nrm((L, KV_LORA_RANK), 0.02)
    w_uk = nrm((L, KV_LORA_RANK, MLA_HEADS * QK_NOPE_DIM), KV_LORA_RANK ** -0.5)
    w_uv = nrm((L, KV_LORA_RANK, MLA_HEADS * V_HEAD_DIM), KV_LORA_RANK ** -0.5 * DEEPNORM_BETA)
    mla_out_g = 1.0 + nrm((L, MLA_OUT), 0.02)
    rwkv_mu = jax.random.uniform(next(keys), (L, RWKV_IN), jnp.float32)
    ramp = jnp.linspace(0.0, 1.0, C) ** 0.85
    rwkv_w0 = -6.5 + 5.0 * ramp[None, :] + nrm((L, C), 0.1)
    rwkv_w2 = nrm((L, DECAY_LORA, C), 0.1 * DECAY_LORA ** -0.5)
    rwkv_a0 = nrm((L, C), 0.1)
    rwkv_a2 = nrm((L, ICLR_LORA, C), 0.1 * ICLR_LORA ** -0.5)
    rwkv_g2 = nrm((L, GATE_LORA, C), GATE_LORA ** -0.5)
    rwkv_k_k = 0.85 + nrm((L, C), 0.02)
    rwkv_k_a = 1.0 + nrm((L, C), 0.02)
    rwkv_r_k = nrm((L, RWKV_HEADS, RWKV_HEAD_DIM), 0.1)
    rwkv_gn_g = 1.0 + nrm((L, C), 0.02)
    rwkv_gn_b = nrm((L, C), 0.02)
    vres_v0 = 1.0 + nrm((L - 1, C), 0.1)
    vres_v1 = nrm((L - 1, C, VRES_LORA), C ** -0.5)
    vres_v2 = nrm((L - 1, VRES_LORA, C), 0.1 * VRES_LORA ** -0.5)
    w_o = nrm((L, MIX_DIM, D), MIX_DIM ** -0.5 * DEEPNORM_BETA)
    ln1_g = 1.0 + nrm((L, D), 0.02)
    ln1_b = nrm((L, D), 0.02)
    router_w = nrm((L, D, E), fan)
    router_b = nrm((L, E), 0.01)
    exp_w1 = nrm((L, E, D, 2 * F), fan)
    exp_b1 = nrm((L, E, 2 * F), 0.02)
    exp_w2 = nrm((L, E, F, D), F ** -0.5 * DEEPNORM_BETA)
    exp_b2 = nrm((L, E, D), 0.02)
    ln2_g = 1.0 + nrm((L, D), 0.02)
    ln2_b = nrm((L, D), 0.02)
    return {'x': x, 'c': c, 'positions': positions, 'emb_ln_g': emb_ln_g, 'emb_ln_b': emb_ln_b,
            'ada_w': ada_w, 'ada_b': ada_b, 'w_in': w_in,
            'q_norm_g': q_norm_g, 'w_uq': w_uq, 'kv_norm_g': kv_norm_g, 'w_uk': w_uk, 'w_uv': w_uv,
            'mla_out_g': mla_out_g, 'rwkv_mu': rwkv_mu, 'rwkv_w0': rwkv_w0, 'rwkv_w2': rwkv_w2,
            'rwkv_a0': rwkv_a0, 'rwkv_a2': rwkv_a2, 'rwkv_g2': rwkv_g2, 'rwkv_k_k': rwkv_k_k,
            'rwkv_k_a': rwkv_k_a, 'rwkv_r_k': rwkv_r_k, 'rwkv_gn_g': rwkv_gn_g, 'rwkv_gn_b': rwkv_gn_b,
            'vres_v0': vres_v0, 'vres_v1': vres_v1, 'vres_v2': vres_v2, 'w_o': w_o,
            'ln1_g': ln1_g, 'ln1_b': ln1_b, 'router_w': router_w, 'router_b': router_b,
            'exp_w1': exp_w1, 'exp_b1': exp_b1, 'exp_w2': exp_w2, 'exp_b2': exp_b2,
            'ln2_g': ln2_g, 'ln2_b': ln2_b}


def reference(x, c, positions, emb_ln_g, emb_ln_b, ada_w, ada_b, w_in,
              q_norm_g, w_uq, kv_norm_g, w_uk, w_uv, mla_out_g,
              rwkv_mu, rwkv_w0, rwkv_w2, rwkv_a0, rwkv_a2, rwkv_g2, rwkv_k_k, rwkv_k_a, rwkv_r_k,
              rwkv_gn_g, rwkv_gn_b, vres_v0, vres_v1, vres_v2, w_o, ln1_g, ln1_b,
              router_w, router_b, exp_w1, exp_b1, exp_w2, exp_b2, ln2_g, ln2_b):
    inv_freq = ROPE_THETA ** (-jnp.arange(0, QK_ROPE_DIM, 2, dtype=jnp.float32) / QK_ROPE_DIM)
    ang = positions.astype(jnp.float32)[..., None] * inv_freq
    cos, sin = jnp.cos(ang), jnp.sin(ang)
    c_act = jax.nn.silu(c)
    x = layer_norm(x, emb_ln_g, emb_ln_b)
    v_first = None
    for i in range(DEPTH):
        mod = c_act @ ada_w[i] + ada_b[i]
        sh1, sc1, gt1, sh2, sc2, gt2 = [m[:, None, :] for m in jnp.split(mod, 6, axis=-1)]
        h = x * (1.0 + sc1) + sh1
        p = h @ w_in[i]
        mla_o = mla_group(p[..., :MLA_IN], cos, sin, q_norm_g[i], w_uq[i], kv_norm_g[i],
                          w_uk[i], w_uv[i], mla_out_g[i])
        vres = None if i == 0 else (vres_v0[i - 1], vres_v1[i - 1], vres_v2[i - 1])
        rwkv_o, v_i = rwkv7_group(p[..., MLA_IN:], rwkv_mu[i], rwkv_w0[i], rwkv_w2[i], rwkv_a0[i],
                                  rwkv_a2[i], rwkv_g2[i], rwkv_k_k[i], rwkv_k_a[i], rwkv_r_k[i],
                                  rwkv_gn_g[i], rwkv_gn_b[i], v_first, vres)
        if i == 0:
            v_first = v_i
        mix = jnp.concatenate([mla_o, rwkv_o], axis=-1) @ w_o[i]
        x = layer_norm(DEEPNORM_ALPHA * x + (1.0 + gt1) * mix, ln1_g[i], ln1_b[i])
        h = x * (1.0 + sc2) + sh2
        ffn = moe_ffn(h, router_w[i], router_b[i], exp_w1[i], exp_b1[i], exp_w2[i], exp_b2[i])
        x = layer_norm(DEEPNORM_ALPHA * x + (1.0 + gt2) * ffn, ln2_g[i], ln2_b[i])
    return x
```

```python
import functools

import jax
import jax.numpy as jnp
import numpy as np
from jax import lax
from jax.experimental import pallas as pl
from jax.experimental.pallas import tpu as pltpu

F32 = jnp.float32
BF16 = jnp.bfloat16
I32 = jnp.int32

D_MODEL = 1024
DEPTH = 2
LANES = 128
SUBLANES = 8

MLA_HEADS = 8
QK_NOPE = 64
QK_ROPE = 32
QK_HEAD = QK_NOPE + QK_ROPE
V_HEAD = 64
Q_LORA = 256
KV_LORA = 128
ROPE_THETA = 10000.0
MLA_OUT = MLA_HEADS * V_HEAD

RWKV_HEADS = 8
RWKV_N = 64
RWKV_DIM = RWKV_HEADS * RWKV_N
DECAY_LORA = 64
ICLR_LORA = 64
GATE_LORA = 128
RWKV_GN_EPS = 64e-5
RWKV_IN = 3 * RWKV_DIM + DECAY_LORA + ICLR_LORA + GATE_LORA
MLA_IN = Q_LORA + KV_LORA + QK_ROPE

N_EXPERTS = 32
TOP_K = 4
SWIGLU_LIMIT = 7.0
SWIGLU_ALPHA = 1.702

DEEPNORM_ALPHA = (2 * DEPTH) ** 0.25
LN_EPS = 1e-5
RMS_EPS = 1e-6

PM_COLS = Q_LORA + KV_LORA + 2 * LANES
CHUNK = 64
VMEM_LIMIT = 56 * 1024 * 1024


def _cparams(sem):
    return pltpu.CompilerParams(dimension_semantics=sem, vmem_limit_bytes=VMEM_LIMIT)


def _bdot(a, b):
    return jnp.dot(a.astype(BF16), b.astype(BF16), preferred_element_type=F32)


def _bdot_nt(a, b):
    return lax.dot_general(a.astype(BF16), b.astype(BF16), (((1,), (1,)), ((), ())),
                           preferred_element_type=F32)


def _split_dot(x, ones_bf16):
    hi = x.astype(BF16)
    lo = (x - hi.astype(F32)).astype(BF16)
    return (jnp.dot(hi, ones_bf16, preferred_element_type=F32)
            + jnp.dot(lo, ones_bf16, preferred_element_type=F32))


def _layer_norm(x, g, b):
    mu = jnp.mean(x, axis=-1, keepdims=True)
    xc = x - mu
    var = jnp.mean(xc * xc, axis=-1, keepdims=True)
    return xc * lax.rsqrt(var + LN_EPS) * g + b


def _rms_norm(x, g):
    return x * lax.rsqrt(jnp.mean(x * x, axis=-1, keepdims=True) + RMS_EPS) * g


def _mod_kernel(c_ref, w_ref, b_ref, o_ref):
    c = c_ref[...]
    c_act = c * jax.nn.sigmoid(c)
    o_ref[...] = jnp.dot(c_act, w_ref[...], preferred_element_type=F32,
                         precision=lax.Precision.HIGHEST) + b_ref[...]


def _modulation(c, ada_w, ada_b):
    b, d = c.shape
    n = ada_w.shape[-1] // d
    return pl.pallas_call(
        _mod_kernel,
        grid=(DEPTH, n),
        in_specs=[pl.BlockSpec((b, d), lambda l, j: (0, 0)),
                  pl.BlockSpec((None, d, d), lambda l, j: (l, 0, j)),
                  pl.BlockSpec((None, 1, d), lambda l, j: (l, 0, j))],
        out_specs=pl.BlockSpec((None, b, d), lambda l, j: (l, 0, j)),
        out_shape=jax.ShapeDtypeStruct((DEPTH, b, n * d), F32),
        compiler_params=_cparams(("arbitrary", "arbitrary")),
        name="modulation",
    )(c, ada_w, ada_b.reshape(DEPTH, 1, n * d))


def _inproj_kernel(x_ref, g_ref, b_ref, sc_ref, sh_ref, w_ref, xn_ref, pm_ref, pr_ref, *, do_ln):
    x = x_ref[...]
    if do_ln:
        x = _layer_norm(x, g_ref[...], b_ref[...])
    xn_ref[...] = x
    h = x * (1.0 + sc_ref[...]) + sh_ref[...]
    p = jnp.dot(h.astype(BF16), w_ref[...], preferred_element_type=F32)
    pm_ref[...] = p[:, :PM_COLS]
    pr_ref[...] = p[:, PM_COLS:]


def _inproj(x, ln_g, ln_b, sc, sh, w, seq, tm, do_ln):
    t, d = x.shape
    per_b = seq // tm
    n = w.shape[1]
    row = lambda i: (i, 0)
    const = lambda i: (0, 0)
    mod = lambda i: (i // per_b, 0, 0)
    return pl.pallas_call(
        functools.partial(_inproj_kernel, do_ln=do_ln),
        grid=(t // tm,),
        in_specs=[pl.BlockSpec((tm, d), row),
                  pl.BlockSpec((1, d), const), pl.BlockSpec((1, d), const),
                  pl.BlockSpec((None, 1, d), mod), pl.BlockSpec((None, 1, d), mod),
                  pl.BlockSpec((d, n), const)],
        out_specs=[pl.BlockSpec((tm, d), row), pl.BlockSpec((tm, PM_COLS), row),
                   pl.BlockSpec((tm, RWKV_IN), row)],
        out_shape=[jax.ShapeDtypeStruct((t, d), F32), jax.ShapeDtypeStruct((t, PM_COLS), F32),
                   jax.ShapeDtypeStruct((t, RWKV_IN), F32)],
        compiler_params=_cparams(("parallel",)),
        name="inproj",
    )(x, ln_g, ln_b, sc, sh, w)


def _mla_prep_kernel(pm_ref, pos_ref, qg_ref, kvg_ref, wq_ref, wkv_ref, invf_ref, sgn_ref,
                     q_ref, k_ref, v_ref):
    pm = pm_ref[...]
    ang = pos_ref[...].astype(F32) * invf_ref[...]
    cos = jnp.cos(ang)
    sin = jnp.sin(ang) * sgn_ref[...]
    qn = _rms_norm(pm[:, :Q_LORA], qg_ref[...])
    q2 = jnp.dot(qn.astype(BF16), wq_ref[...], preferred_element_type=F32)
    kvn = _rms_norm(pm[:, Q_LORA:Q_LORA + KV_LORA], kvg_ref[...])
    kv = jnp.dot(kvn.astype(BF16), wkv_ref[...], preferred_element_type=F32)
    off = Q_LORA + KV_LORA
    k_rot = pm[:, off:off + LANES] * cos + pm[:, off + LANES:off + 2 * LANES] * sin
    scale = QK_HEAD ** -0.5
    hw = MLA_HEADS * LANES
    for h in range(MLA_HEADS):
        sl = slice(h * LANES, (h + 1) * LANES)
        sl2 = slice(hw + h * LANES, hw + (h + 1) * LANES)
        q_ref[:, sl] = ((q2[:, sl] * cos + q2[:, sl2] * sin) * scale).astype(BF16)
        k_ref[:, sl] = (kv[:, sl] + k_rot).astype(BF16)
    v_ref[...] = kv[:, hw:].astype(BF16)


def _mla_prep(pm, pos, q_g, kv_g, wq2, wkv, invf, sgn, tm):
    t = pm.shape[0]
    hw = MLA_HEADS * LANES
    row = lambda i: (i, 0)
    const = lambda i: (0, 0)
    return pl.pallas_call(
        _mla_prep_kernel,
        grid=(t // tm,),
        in_specs=[pl.BlockSpec((tm, PM_COLS), row), pl.BlockSpec((tm, 1), row),
                  pl.BlockSpec((1, Q_LORA), const), pl.BlockSpec((1, KV_LORA), const),
                  pl.BlockSpec(wq2.shape, const), pl.BlockSpec(wkv.shape, const),
                  pl.BlockSpec((1, LANES), const), pl.BlockSpec((1, LANES), const)],
        out_specs=[pl.BlockSpec((tm, hw), row), pl.BlockSpec((tm, hw), row),
                   pl.BlockSpec((tm, MLA_OUT), row)],
        out_shape=[jax.ShapeDtypeStruct((t, hw), BF16), jax.ShapeDtypeStruct((t, hw), BF16),
                   jax.ShapeDtypeStruct((t, MLA_OUT), BF16)],
        compiler_params=_cparams(("parallel",)),
        name="mla_prep",
    )(pm, pos, q_g, kv_g, wq2, wkv, invf, sgn)


def _attn_kernel(q_ref, k_ref, v_ref, o_ref, *, tq):
    qi = pl.program_id(2)
    lane = lax.broadcasted_iota(I32, (tq, LANES), 1)
    rows = lax.broadcasted_iota(I32, (tq, tq), 0)
    cols = lax.broadcasted_iota(I32, (tq, tq), 1)
    causal = cols <= rows
    outs = []
    for h in range(2):
        q = q_ref[:, h * LANES:(h + 1) * LANES]

        def block(j, carry, masked):
            m, l, acc = carry
            start = pl.multiple_of(j * tq, tq)
            k = k_ref[pl.ds(start, tq), h * LANES:(h + 1) * LANES]
            v = v_ref[pl.ds(start, tq), :]
            s = lax.dot_general(q, k, (((1,), (1,)), ((), ())), preferred_element_type=F32)
            if masked:
                s = jnp.where(causal, s, -jnp.inf)
            m_new = jnp.maximum(m, jnp.max(s, axis=-1, keepdims=True))
            a = jnp.exp(m - m_new)
            p = jnp.exp(s - m_new)
            l_new = a * l + jnp.sum(p, axis=-1, keepdims=True)
            acc_new = a * acc + jnp.dot(p.astype(BF16), v, preferred_element_type=F32)
            return m_new, l_new, acc_new

        init = (jnp.full((tq, 1), -jnp.inf, F32), jnp.zeros((tq, 1), F32),
                jnp.zeros((tq, LANES), F32))
        carry = lax.fori_loop(0, qi, functools.partial(block, masked=False), init)
        m, l, acc = block(qi, carry, True)
        outs.append(acc / l)
    o_ref[...] = jnp.where(lane < V_HEAD, outs[0], outs[1])


def _attention(q, k, v, batch, seq, tq):
    t = q.shape[0]
    nq = seq // tq
    return pl.pallas_call(
        functools.partial(_attn_kernel, tq=tq),
        grid=(batch, MLA_HEADS // 2, nq),
        in_specs=[pl.BlockSpec((tq, 2 * LANES), lambda b, h, i: (b * nq + i, h)),
                  pl.BlockSpec((seq, 2 * LANES), lambda b, h, i: (b, h)),
                  pl.BlockSpec((seq, LANES), lambda b, h, i: (b, h))],
        out_specs=pl.BlockSpec((tq, LANES), lambda b, h, i: (b * nq + i, h)),
        out_shape=jax.ShapeDtypeStruct((t, MLA_OUT), F32),
        compiler_params=_cparams(("parallel", "parallel", "arbitrary")),
        name="attention",
    )(q, k, v)


def _head_ones(n):
    r = lax.broadcasted_iota(I32, (n, n), 0) // RWKV_N
    c = lax.broadcasted_iota(I32, (n, n), 1) // RWKV_N
    return (r == c).astype(BF16)


def _rwkv_prep_kernel(*refs, has_vres):
    if has_vres:
        (p_ref, prev_ref, mu_ref, w0_ref, w2_ref, a0_ref, a2_ref, g2_ref, kk_ref, ka_ref, rk_ref,
         vf_ref, v0_ref, v1_ref, v2_ref,
         r_out, lw_out, k_out, v_out, al_out, be_out, g_out, bonus_out) = refs
    else:
        (p_ref, prev_ref, mu_ref, w0_ref, w2_ref, a0_ref, a2_ref, g2_ref, kk_ref, ka_ref, rk_ref,
         r_out, lw_out, k_out, v_out, al_out, be_out, g_out, bonus_out) = refs
    i = pl.program_id(1)
    p = p_ref[...]
    tm = p.shape[0]
    prev_row = jnp.where(i == 0, 0.0, prev_ref[SUBLANES - 1:SUBLANES, :])
    row = lax.broadcasted_iota(I32, p.shape, 0)
    p_prev = jnp.where(row == 0, prev_row, pltpu.roll(p, 1, 0))
    p = p + mu_ref[...] * (p_prev - p)
    c = RWKV_DIM
    r = p[:, :c]
    k = p[:, c:2 * c]
    v = p[:, 2 * c:3 * c]
    wd = p[:, 3 * c:3 * c + DECAY_LORA]
    ad = p[:, 3 * c + DECAY_LORA:3 * c + DECAY_LORA + ICLR_LORA]
    gd = p[:, 3 * c + DECAY_LORA + ICLR_LORA:]
    z = w0_ref[...] + _bdot(jnp.tanh(wd), w2_ref[...])
    y = -z
    softplus = jnp.maximum(y, 0.0) + jnp.log(1.0 + jnp.exp(-jnp.abs(y)))
    lw_out[...] = -jnp.exp(-softplus - 0.5)
    a = jax.nn.sigmoid(a0_ref[...] + _bdot(ad, a2_ref[...]))
    g_out[...] = _bdot(jax.nn.sigmoid(gd), g2_ref[...])
    if has_vres:
        mix = jax.nn.sigmoid(v0_ref[...] + _bdot(_bdot(v, v1_ref[...]), v2_ref[...]))
        v = v + (vf_ref[...] - v) * mix
    ones = _head_ones(c)
    kk = k * kk_ref[...]
    norm = jnp.sqrt(_split_dot(kk * kk, ones))
    kk = kk / jnp.maximum(norm, 1e-12)
    k = k * (1.0 + (a - 1.0) * ka_ref[...])
    r_out[...] = r
    k_out[...] = k
    v_out[...] = v
    al_out[...] = -kk
    be_out[...] = kk * a
    bonus_out[...] = _split_dot(r * k * rk_ref[...], ones) * v


def _rwkv_prep(pr, mu, w0, w2, a0, a2, g2, k_k, k_a, r_k, vres, batch, seq, tm):
    t = pr.shape[0]
    per_b = seq // tm
    c = RWKV_DIM
    row = lambda b, i: (b * per_b + i, 0)
    const = lambda b, i: (0, 0)
    prev = lambda b, i: (jnp.maximum((b * per_b + i) * (tm // SUBLANES) - 1, 0), 0)
    vec = lambda n: pl.BlockSpec((1, n), const)
    in_specs = [pl.BlockSpec((tm, RWKV_IN), row), pl.BlockSpec((SUBLANES, RWKV_IN), prev),
                vec(RWKV_IN), vec(c), pl.BlockSpec(w2.shape, const), vec(c),
                pl.BlockSpec(a2.shape, const), pl.BlockSpec(g2.shape, const), vec(c), vec(c), vec(c)]
    args = [pr, pr, mu, w0, w2, a0, a2, g2, k_k, k_a, r_k]
    if vres is not None:
        v_first, v0, v1, v2 = vres
        in_specs += [pl.BlockSpec((tm, c), row), vec(c), pl.BlockSpec(v1.shape, const),
                     pl.BlockSpec(v2.shape, const)]
        args += [v_first, v0, v1, v2]
    out = jax.ShapeDtypeStruct((t, c), F32)
    return pl.pallas_call(
        functools.partial(_rwkv_prep_kernel, has_vres=vres is not None),
        grid=(batch, per_b),
        in_specs=in_specs,
        out_specs=[pl.BlockSpec((tm, c), row)] * 8,
        out_shape=[out] * 8,
        compiler_params=_cparams(("parallel", "parallel")),
        name="rwkv_prep",
    )(*args)


def _rwkv_rec_kernel(r_ref, lw_ref, k_ref, v_ref, al_ref, be_ref, o_ref, st_ref):
    cs = CHUNK

    @pl.when(pl.program_id(1) == 0)
    def _():
        st_ref[...] = jnp.zeros_like(st_ref)

    row = lax.broadcasted_iota(I32, (cs, LANES), 0)
    lane = lax.broadcasted_iota(I32, (cs, LANES), 1)
    col = lane % cs
    strict = col < row
    incl = col <= row
    eye_pair = (col == row).astype(F32)
    r2 = lax.broadcasted_iota(I32, (LANES, LANES), 0)
    l2 = lax.broadcasted_iota(I32, (LANES, LANES), 1)
    same_head = (r2 // RWKV_N) == (l2 // RWKV_N)
    eye128 = r2 == l2
    tri = (lax.broadcasted_iota(I32, (cs, cs), 1) <= lax.broadcasted_iota(I32, (cs, cs), 0)).astype(BF16)

    def bd(x):
        return jnp.where(same_head, jnp.concatenate([x, x], axis=0), 0.0).astype(BF16)

    for p in range(RWKV_HEADS // 2):
        sl = slice(p * LANES, (p + 1) * LANES)
        r = r_ref[:, sl]
        lw = lw_ref[:, sl]
        k = k_ref[:, sl]
        v = v_ref[:, sl]
        al = al_ref[:, sl]
        be = be_ref[:, sl]
        cum = _split_dot_left(tri, lw)
        cum_end = cum[cs - 1:cs, :]
        e_neg = jnp.exp(-cum)
        e_end = jnp.exp(cum_end - cum)
        a_bar = al * jnp.exp(cum - lw)
        r_bar = r * jnp.exp(cum)
        b_bar = be * e_neg
        k_bar = k * e_neg
        b_til = be * e_end
        k_til = k * e_end
        lhs = jnp.concatenate([a_bar, r_bar], axis=0)
        rhs = jnp.concatenate([bd(b_bar), bd(k_bar)], axis=0)
        gram = _bdot_nt(lhs, rhs)
        l_ab = jnp.where(strict, gram[:cs, :LANES], 0.0)
        l_ak = jnp.where(strict, gram[:cs, LANES:], 0.0)
        m_rb = jnp.where(incl, gram[cs:, :LANES], 0.0)
        m_rk = jnp.where(incl, gram[cs:, LANES:], 0.0)
        x = eye_pair + l_ab
        lp = l_ab
        for _ in range(5):
            lp = _bdot(lp, bd(lp))
            x = x + _bdot(x, bd(lp))
        bdv = bd(v)
        w1 = _bdot(x, bd(a_bar))
        u0 = _bdot(x, bd(_bdot(l_ak, bdv)))
        mv = _bdot(m_rk, bdv)
        sb = st_ref[p]
        ws = _bdot(jnp.concatenate([w1, r_bar], axis=0), sb)
        u = ws[:cs] + u0
        o_ref[:, sl] = ws[cs:] + _bdot(m_rb, bd(u)) + mv
        upd = _bdot(jnp.concatenate([b_til, k_til], axis=0).T, jnp.concatenate([u, v], axis=0))
        pc_col = jnp.sum(jnp.where(eye128, jnp.exp(cum_end), 0.0), axis=1, keepdims=True)
        st_ref[p] = pc_col * sb + jnp.where(same_head, upd, 0.0)


def _split_dot_left(ones_bf16, x):
    hi = x.astype(BF16)
    lo = (x - hi.astype(F32)).astype(BF16)
    return (jnp.dot(ones_bf16, hi, preferred_element_type=F32)
            + jnp.dot(ones_bf16, lo, preferred_element_type=F32))


def _rwkv_recurrence(r, lw, k, v, al, be, batch, seq):
    t, c = r.shape
    nc = seq // CHUNK
    blk = pl.BlockSpec((CHUNK, c), lambda b, j: (b * nc + j, 0))
    return pl.pallas_call(
        _rwkv_rec_kernel,
        grid=(batch, nc),
        in_specs=[blk] * 6,
        out_specs=blk,
        out_shape=jax.ShapeDtypeStruct((t, c), F32),
        scratch_shapes=[pltpu.VMEM((RWKV_HEADS // 2, LANES, LANES), F32)],
        compiler_params=_cparams(("parallel", "arbitrary")),
        name="rwkv_recurrence",
    )(r, lw, k, v, al, be)


def _outproj_kernel(x_ref, att_ref, rec_ref, bonus_ref, g_ref, og_ref, gng_ref, gnb_ref, wo_ref,
                    gt_ref, lng_ref, lnb_ref, sc_ref, sh_ref, rw_ref, rb_ref,
                    x1_ref, h2_ref, idx_ref, gate_ref, rank_ref, cnt_ref, run_ref):
    step = pl.program_id(0)

    @pl.when(step == 0)
    def _():
        run_ref[...] = jnp.zeros_like(run_ref)

    tm = x_ref.shape[0]
    mla = _rms_norm(att_ref[...], og_ref[...])
    ones = _head_ones(RWKV_DIM)
    o = rec_ref[...]
    mean = _split_dot(o, ones) * (1.0 / RWKV_N)
    oc = o - mean
    var = _split_dot(oc * oc, ones) * (1.0 / RWKV_N)
    rw = (oc * lax.rsqrt(var + RWKV_GN_EPS) * gng_ref[...] + gnb_ref[...] + bonus_ref[...]) * g_ref[...]
    mix = (jnp.dot(mla.astype(BF16), wo_ref[:MLA_OUT, :], preferred_element_type=F32)
           + jnp.dot(rw.astype(BF16), wo_ref[MLA_OUT:, :], preferred_element_type=F32))
    x1 = _layer_norm(DEEPNORM_ALPHA * x_ref[...] + (1.0 + gt_ref[...]) * mix, lng_ref[...], lnb_ref[...])
    x1_ref[...] = x1
    h2 = x1 * (1.0 + sc_ref[...]) + sh_ref[...]
    for j in range(D_MODEL // LANES):
        h2_ref[pl.ds(j, tm, stride=SUBLANES), :] = h2[:, j * LANES:(j + 1) * LANES]
    logits = jnp.dot(h2, rw_ref[...], preferred_element_type=F32,
                     precision=lax.Precision.HIGHEST) + rb_ref[...]
    e_iota = lax.broadcasted_iota(I32, logits.shape, 1)
    lane = lax.broadcasted_iota(I32, (tm, LANES), 1)
    work = logits
    vals, hots = [], []
    idx_out = jnp.zeros((tm, LANES), I32)
    for kk in range(TOP_K):
        m = jnp.max(work, axis=-1, keepdims=True)
        sel = jnp.min(jnp.where(work == m, e_iota, N_EXPERTS), axis=-1, keepdims=True)
        hot = e_iota == sel
        vals.append(m)
        hots.append(hot)
        idx_out = jnp.where(lane == kk, sel, idx_out)
        work = jnp.where(hot, -jnp.inf, work)
    exps = [jnp.exp(vv - vals[0]) for vv in vals]
    denom = exps[0] + exps[1] + exps[2] + exps[3]
    gate_out = jnp.zeros((tm, LANES), F32)
    for kk in range(TOP_K):
        gate_out = jnp.where(lane == kk, exps[kk] / denom, gate_out)
    any_hot = (hots[0] | hots[1] | hots[2] | hots[3]).astype(BF16)
    tri = (lax.broadcasted_iota(I32, (tm, tm), 1) < lax.broadcasted_iota(I32, (tm, tm), 0)).astype(BF16)
    before = jnp.dot(tri, any_hot, preferred_element_type=F32) + run_ref[...]
    rank_out = jnp.zeros((tm, LANES), I32)
    for kk in range(TOP_K):
        rk = jnp.sum(jnp.where(hots[kk], before, 0.0), axis=-1, keepdims=True).astype(I32)
        rank_out = jnp.where(lane == kk, rk, rank_out)
    run_ref[...] = run_ref[...] + jnp.sum(any_hot.astype(F32), axis=0, keepdims=True)
    idx_ref[...] = idx_out
    gate_ref[...] = gate_out
    rank_ref[...] = rank_out
    cnt_ref[...] = jnp.broadcast_to(run_ref[...], cnt_ref.shape)


def _outproj(x, att, rec, bonus, g, out_g, gn_g, gn_b, wo, gt, ln_g, ln_b, sc, sh, rw, rb, seq, tm):
    t, d = x.shape
    per_b = seq // tm
    row = lambda i: (i, 0)
    const = lambda i: (0, 0)
    mod = lambda i: (i // per_b, 0, 0)
    vec = lambda n: pl.BlockSpec((1, n), const)
    half = lambda: pl.BlockSpec((tm, RWKV_DIM), row)
    wide = jax.ShapeDtypeStruct((t, LANES), I32)
    return pl.pallas_call(
        _outproj_kernel,
        grid=(t // tm,),
        in_specs=[pl.BlockSpec((tm, d), row), half(), half(), half(), half(),
                  vec(MLA_OUT), vec(RWKV_DIM), vec(RWKV_DIM), pl.BlockSpec(wo.shape, const),
                  pl.BlockSpec((None, 1, d), mod), vec(d), vec(d),
                  pl.BlockSpec((None, 1, d), mod), pl.BlockSpec((None, 1, d), mod),
                  pl.BlockSpec(rw.shape, const), vec(N_EXPERTS)],
        out_specs=[pl.BlockSpec((tm, d), row), pl.BlockSpec((tm * SUBLANES, LANES), row),
                   pl.BlockSpec((tm, LANES), row), pl.BlockSpec((tm, LANES), row),
                   pl.BlockSpec((tm, LANES), row), pl.BlockSpec((SUBLANES, N_EXPERTS), const)],
        out_shape=[jax.ShapeDtypeStruct((t, d), F32),
                   jax.ShapeDtypeStruct((t * SUBLANES, LANES), F32),
                   wide, jax.ShapeDtypeStruct((t, LANES), F32), wide,
                   jax.ShapeDtypeStruct((SUBLANES, N_EXPERTS), F32)],
        scratch_shapes=[pltpu.VMEM((1, N_EXPERTS), F32)],
        compiler_params=_cparams(("arbitrary",)),
        name="outproj_router",
    )(x, att, rec, bonus, g, out_g, gn_g, gn_b, wo, gt, ln_g, ln_b, sc, sh, rw, rb)


def _dispatch_kernel(dest_ref, h_hbm, zeros_hbm, xs_hbm, sem, *, td):
    del zeros_hbm
    base = pl.program_id(0) * td
    n = td * TOP_K

    def copy(a):
        src = pl.multiple_of((base + a // TOP_K) * SUBLANES, SUBLANES)
        dst = pl.multiple_of(dest_ref[0, a] * SUBLANES, SUBLANES)
        return pltpu.make_async_copy(h_hbm.at[pl.ds(src, SUBLANES)], xs_hbm.at[pl.ds(dst, SUBLANES)], sem)

    def start(a, carry):
        copy(a).start()
        return carry

    def wait(a, carry):
        copy(a).wait()
        return carry

    lax.fori_loop(0, n, start, 0)
    lax.fori_loop(0, n, wait, 0)


def _dispatch(dest, h_tiles, n_slots, td):
    t = dest.shape[0]
    dest3 = dest.reshape(t // td, 1, td * TOP_K)
    zeros = jnp.zeros((n_slots * SUBLANES, LANES), F32)
    return pl.pallas_call(
        functools.partial(_dispatch_kernel, td=td),
        grid=(t // td,),
        in_specs=[pl.BlockSpec((None, 1, td * TOP_K), lambda i: (i, 0, 0), memory_space=pltpu.SMEM),
                  pl.BlockSpec(memory_space=pl.ANY), pl.BlockSpec(memory_space=pl.ANY)],
        out_specs=pl.BlockSpec(memory_space=pl.ANY),
        out_shape=jax.ShapeDtypeStruct((n_slots * SUBLANES, LANES), F32),
        scratch_shapes=[pltpu.SemaphoreType.DMA(())],
        input_output_aliases={2: 0},
        compiler_params=_cparams(("arbitrary",)),
        name="dispatch",
    )(dest3, h_tiles, zeros)


def _ffn_kernel(be_ref, nused_ref, x_ref, w1_ref, b1_ref, w2_ref, b2_ref, y_ref, *, bm):
    del be_ref
    i = pl.program_id(0)
    f = w2_ref.shape[0]

    @pl.when(i < nused_ref[0])
    def _():
        x = jnp.concatenate([x_ref[pl.ds(j, bm, stride=SUBLANES), :] for j in range(D_MODEL // LANES)],
                            axis=1)
        hh = jnp.dot(x.astype(BF16), w1_ref[...], preferred_element_type=F32) + b1_ref[...]
        x_glu = jnp.minimum(hh[:, :f], SWIGLU_LIMIT)
        x_lin = jnp.clip(hh[:, f:], -SWIGLU_LIMIT, SWIGLU_LIMIT)
        u = x_glu * jax.nn.sigmoid(SWIGLU_ALPHA * x_glu) * (x_lin + 1.0)
        y = jnp.dot(u.astype(BF16), w2_ref[...], preferred_element_type=F32) + b2_ref[...]
        for j in range(D_MODEL // LANES):
            y_ref[pl.ds(j, bm, stride=SUBLANES), :] = y[:, j * LANES:(j + 1) * LANES]

    @pl.when(i >= nused_ref[0])
    def _():
        y_ref[...] = jnp.zeros_like(y_ref)


def _expert_ffn(block_expert, n_used, xs, w1, b1, w2, b2, bm):
    n_blocks = block_expert.shape[0]
    d = w1.shape[1]
    f = w2.shape[1]
    grid_spec = pltpu.PrefetchScalarGridSpec(
        num_scalar_prefetch=2,
        grid=(n_blocks,),
        in_specs=[pl.BlockSpec((bm * SUBLANES, LANES), lambda i, be, nu: (i, 0)),
                  pl.BlockSpec((None, d, 2 * f), lambda i, be, nu: (be[i], 0, 0)),
                  pl.BlockSpec((None, 1, 2 * f), lambda i, be, nu: (be[i], 0, 0)),
                  pl.BlockSpec((None, f, d), lambda i, be, nu: (be[i], 0, 0)),
                  pl.BlockSpec((None, 1, d), lambda i, be, nu: (be[i], 0, 0))],
        out_specs=pl.BlockSpec((bm * SUBLANES, LANES), lambda i, be, nu: (i, 0)),
    )
    return pl.pallas_call(
        functools.partial(_ffn_kernel, bm=bm),
        grid_spec=grid_spec,
        out_shape=jax.ShapeDtypeStruct(xs.shape, F32),
        compiler_params=_cparams(("arbitrary",)),
        name="expert_ffn",
    )(block_expert, n_used, xs, w1, b1, w2, b2)


def _combine_kernel(dest_ref, y_hbm, x_ref, gate_ref, gt_ref, lng_ref, lnb_ref, o_ref, buf, sem, *, tc):
    n = tc * TOP_K

    def copy(a):
        src = pl.multiple_of(dest_ref[0, a] * SUBLANES, SUBLANES)
        dst = pl.multiple_of(((a % TOP_K) * tc + a // TOP_K) * SUBLANES, SUBLANES)
        return pltpu.make_async_copy(y_hbm.at[pl.ds(src, SUBLANES)], buf.at[pl.ds(dst, SUBLANES)], sem)

    def start(a, carry):
        copy(a).start()
        return carry

    def wait(a, carry):
        copy(a).wait()
        return carry

    lax.fori_loop(0, n, start, 0)
    lax.fori_loop(0, n, wait, 0)
    gates = gate_ref[...]
    pieces = []
    for j in range(D_MODEL // LANES):
        acc = jnp.zeros((tc, LANES), F32)
        for kk in range(TOP_K):
            rows = buf[pl.ds(kk * tc * SUBLANES + j, tc, stride=SUBLANES), :]
            acc = acc + rows * gates[:, kk:kk + 1]
        pieces.append(acc)
    ffn = jnp.concatenate(pieces, axis=1)
    o_ref[...] = _layer_norm(DEEPNORM_ALPHA * x_ref[...] + (1.0 + gt_ref[...]) * ffn,
                             lng_ref[...], lnb_ref[...])


def _combine(dest, y, x1, gates, gt, ln_g, ln_b, seq, tc):
    t, d = x1.shape
    per_b = seq // tc
    dest3 = dest.reshape(t // tc, 1, tc * TOP_K)
    row = lambda i: (i, 0)
    const = lambda i: (0, 0)
    return pl.pallas_call(
        functools.partial(_combine_kernel, tc=tc),
        grid=(t // tc,),
        in_specs=[pl.BlockSpec((None, 1, tc * TOP_K), lambda i: (i, 0, 0), memory_space=pltpu.SMEM),
                  pl.BlockSpec(memory_space=pl.ANY),
                  pl.BlockSpec((tc, d), row), pl.BlockSpec((tc, LANES), row),
                  pl.BlockSpec((None, 1, d), lambda i: (i // per_b, 0, 0)),
                  pl.BlockSpec((1, d), const), pl.BlockSpec((1, d), const)],
        out_specs=pl.BlockSpec((tc, d), row),
        out_shape=jax.ShapeDtypeStruct((t, d), F32),
        scratch_shapes=[pltpu.VMEM((TOP_K * tc * SUBLANES, LANES), F32), pltpu.SemaphoreType.DMA(())],
        compiler_params=_cparams(("arbitrary",)),
        name="combine",
    )(dest3, y, x1, gates, gt, ln_g, ln_b)


def _rope_cols():
    half = QK_ROPE // 2
    return np.arange(half) * 2, np.arange(half) * 2 + 1


def _build_w_in(w_in):
    d = w_in.shape[0]
    even, odd = _rope_cols()
    kr = w_in[:, Q_LORA + KV_LORA:MLA_IN]
    zeros = lambda n: jnp.zeros((d, n), w_in.dtype)
    placed = jnp.concatenate([zeros(QK_NOPE), kr[:, even], kr[:, odd], zeros(LANES - QK_HEAD)], axis=1)
    swapped = jnp.concatenate([zeros(QK_NOPE), kr[:, odd], kr[:, even], zeros(LANES - QK_HEAD)], axis=1)
    return jnp.concatenate([w_in[:, :Q_LORA + KV_LORA], placed, swapped, w_in[:, MLA_IN:]],
                           axis=1).astype(BF16)


def _build_w_uq(w_uq):
    even, odd = _rope_cols()
    w = w_uq.reshape(Q_LORA, MLA_HEADS, QK_HEAD)
    nope = w[:, :, :QK_NOPE]
    rot = w[:, :, QK_NOPE:]
    pad = jnp.zeros((Q_LORA, MLA_HEADS, LANES - QK_HEAD), w_uq.dtype)
    zero_nope = jnp.zeros_like(nope)
    placed = jnp.concatenate([nope, rot[:, :, even], rot[:, :, odd], pad], axis=2)
    swapped = jnp.concatenate([zero_nope, rot[:, :, odd], rot[:, :, even], pad], axis=2)
    return jnp.concatenate([placed.reshape(Q_LORA, -1), swapped.reshape(Q_LORA, -1)], axis=1).astype(BF16)


def _build_w_ukv(w_uk, w_uv):
    w = w_uk.reshape(KV_LORA, MLA_HEADS, QK_NOPE)
    pad = jnp.zeros((KV_LORA, MLA_HEADS, LANES - QK_NOPE), w_uk.dtype)
    placed = jnp.concatenate([w, pad], axis=2).reshape(KV_LORA, -1)
    return jnp.concatenate([placed, w_uv], axis=1).astype(BF16)


def _rope_rows():
    half = QK_ROPE // 2
    inv = ROPE_THETA ** (-np.arange(0, QK_ROPE, 2, dtype=np.float32) / QK_ROPE)
    invf = np.zeros((1, LANES), np.float32)
    sgn = np.zeros((1, LANES), np.float32)
    invf[0, QK_NOPE:QK_NOPE + half] = inv
    invf[0, QK_NOPE + half:QK_HEAD] = inv
    sgn[0, QK_NOPE:QK_NOPE + half] = -1.0
    sgn[0, QK_NOPE + half:QK_HEAD] = 1.0
    return jnp.asarray(invf), jnp.asarray(sgn)


def kernel(x, c, positions, emb_ln_g, emb_ln_b, ada_w, ada_b, w_in, q_norm_g, w_uq, kv_norm_g, w_uk, w_uv, mla_out_g, rwkv_mu, rwkv_w0, rwkv_w2, rwkv_a0, rwkv_a2, rwkv_g2, rwkv_k_k, rwkv_k_a, rwkv_r_k, rwkv_gn_g, rwkv_gn_b, vres_v0, vres_v1, vres_v2, w_o, ln1_g, ln1_b, router_w, router_b, exp_w1, exp_b1, exp_w2, exp_b2, ln2_g, ln2_b):
    batch, seq, d = x.shape
    t = batch * seq
    tm = min(512, seq)
    tq = min(256, seq)
    tr = min(256, seq)
    tc = min(128, seq)
    bm = 512
    n_blocks = (t * TOP_K) // bm + N_EXPERTS
    n_slots = n_blocks * bm

    row = lambda a: a.reshape(1, -1)
    mod = _modulation(c, ada_w, ada_b)
    invf, sgn = _rope_rows()
    pos = positions.reshape(t, 1)
    xf = x.reshape(t, d)
    v_first = None
    for i in range(DEPTH):
        sh1, sc1, gt1, sh2, sc2, gt2 = [m.reshape(batch, 1, d) for m in jnp.split(mod[i], 6, axis=-1)]
        xn, pm, pr = _inproj(xf, row(emb_ln_g), row(emb_ln_b), sc1, sh1, _build_w_in(w_in[i]),
                             seq, tm, do_ln=(i == 0))
        q, k, v = _mla_prep(pm, pos, row(q_norm_g[i]), row(kv_norm_g[i]), _build_w_uq(w_uq[i]),
                            _build_w_ukv(w_uk[i], w_uv[i]), invf, sgn, tm)
        att = _attention(q, k, v, batch, seq, tq)
        vres = None if i == 0 else (v_first, row(vres_v0[i - 1]), vres_v1[i - 1].astype(BF16),
                                    vres_v2[i - 1].astype(BF16))
        r_, lw, k_, v_, al, be, g_, bonus = _rwkv_prep(
            pr, row(rwkv_mu[i]), row(rwkv_w0[i]), rwkv_w2[i].astype(BF16), row(rwkv_a0[i]),
            rwkv_a2[i].astype(BF16), rwkv_g2[i].astype(BF16), row(rwkv_k_k[i]), row(rwkv_k_a[i]),
            row(rwkv_r_k[i]), vres, batch, seq, tm)
        if i == 0:
            v_first = v_
        rec = _rwkv_recurrence(r_, lw, k_, v_, al, be, batch, seq)
        x1, h2, idx, gates, rank, counts = _outproj(
            xn, att, rec, bonus, g_, row(mla_out_g[i]), row(rwkv_gn_g[i]), row(rwkv_gn_b[i]),
            w_o[i].astype(BF16), gt1, row(ln1_g[i]), row(ln1_b[i]), sc2, sh2, router_w[i],
            row(router_b[i]), seq, tr)
        cnt = counts[0].astype(I32)
        padded = (cnt + bm - 1) // bm * bm
        pad_ends = jnp.cumsum(padded)
        pad_starts = pad_ends - padded
        dest = pad_starts[idx[:, :TOP_K]] + rank[:, :TOP_K]
        block_expert = jnp.minimum(
            jnp.searchsorted(pad_ends, jnp.arange(n_blocks, dtype=I32) * bm, side='right'),
            N_EXPERTS - 1).astype(I32)
        n_used = (pad_ends[-1:] // bm).astype(I32)
        xs = _dispatch(dest, h2, n_slots, tm)
        w1 = jnp.concatenate([exp_w1[i][:, :, 0::2], exp_w1[i][:, :, 1::2]], axis=-1).astype(BF16)
        b1 = jnp.concatenate([exp_b1[i][:, 0::2], exp_b1[i][:, 1::2]], axis=-1)[:, None, :]
        y = _expert_ffn(block_expert, n_used, xs, w1, b1, exp_w2[i].astype(BF16),
                        exp_b2[i][:, None, :], bm)
        xf = _combine(dest, y, x1, gates, gt2, row(ln2_g[i]), row(ln2_b[i]), seq, tc)
    return xf.reshape(batch, seq, d)
```

```python
import functools

import jax
import jax.numpy as jnp
import numpy as np
from jax import lax
from jax.experimental import pallas as pl
from jax.experimental.pallas import tpu as pltpu

F32 = jnp.float32
BF16 = jnp.bfloat16
I32 = jnp.int32

D_MODEL = 1024
DEPTH = 2
LANES = 128
SUBLANES = 8

MLA_HEADS = 8
QK_NOPE = 64
QK_ROPE = 32
QK_HEAD = QK_NOPE + QK_ROPE
V_HEAD = 64
Q_LORA = 256
KV_LORA = 128
ROPE_THETA = 10000.0
MLA_OUT = MLA_HEADS * V_HEAD

RWKV_HEADS = 8
RWKV_N = 64
RWKV_GROUP = 4
RWKV_DIM = RWKV_HEADS * RWKV_N
DECAY_LORA = 64
ICLR_LORA = 64
GATE_LORA = 128
RWKV_GN_EPS = 64e-5
RWKV_IN = 3 * RWKV_DIM + DECAY_LORA + ICLR_LORA + GATE_LORA
MLA_IN = Q_LORA + KV_LORA + QK_ROPE

N_EXPERTS = 32
TOP_K = 4
SWIGLU_LIMIT = 7.0
SWIGLU_ALPHA = 1.702

DEEPNORM_ALPHA = (2 * DEPTH) ** 0.25
LN_EPS = 1e-5
RMS_EPS = 1e-6

PM_COLS = Q_LORA + KV_LORA + 2 * LANES
CHUNK = 64
VMEM_LIMIT = 56 * 1024 * 1024


def _cparams(sem):
    return pltpu.CompilerParams(dimension_semantics=sem, vmem_limit_bytes=VMEM_LIMIT)


def _bdot(a, b):
    return jnp.dot(a.astype(BF16), b.astype(BF16), preferred_element_type=F32)


def _bdot_nt(a, b):
    return lax.dot_general(a.astype(BF16), b.astype(BF16), (((1,), (1,)), ((), ())),
                           preferred_element_type=F32)


def _split_dot(x, ones_bf16):
    hi = x.astype(BF16)
    lo = (x - hi.astype(F32)).astype(BF16)
    return (jnp.dot(hi, ones_bf16, preferred_element_type=F32)
            + jnp.dot(lo, ones_bf16, preferred_element_type=F32))


def _layer_norm(x, g, b):
    mu = jnp.mean(x, axis=-1, keepdims=True)
    xc = x - mu
    var = jnp.mean(xc * xc, axis=-1, keepdims=True)
    return xc * lax.rsqrt(var + LN_EPS) * g + b


def _rms_norm(x, g):
    return x * lax.rsqrt(jnp.mean(x * x, axis=-1, keepdims=True) + RMS_EPS) * g


def _mod_kernel(c_ref, w_ref, b_ref, o_ref):
    c = c_ref[...]
    c_act = c * jax.nn.sigmoid(c)
    o_ref[...] = jnp.dot(c_act, w_ref[...], preferred_element_type=F32,
                         precision=lax.Precision.HIGHEST) + b_ref[...]


def _modulation(c, ada_w, ada_b):
    b, d = c.shape
    n = ada_w.shape[-1] // d
    return pl.pallas_call(
        _mod_kernel,
        grid=(DEPTH, n),
        in_specs=[pl.BlockSpec((b, d), lambda l, j: (0, 0)),
                  pl.BlockSpec((None, d, d), lambda l, j: (l, 0, j)),
                  pl.BlockSpec((None, 1, d), lambda l, j: (l, 0, j))],
        out_specs=pl.BlockSpec((None, b, d), lambda l, j: (l, 0, j)),
        out_shape=jax.ShapeDtypeStruct((DEPTH, b, n * d), F32),
        compiler_params=_cparams(("arbitrary", "arbitrary")),
        name="modulation",
    )(c, ada_w, ada_b.reshape(DEPTH, 1, n * d))


def _inproj_kernel(x_ref, g_ref, b_ref, sc_ref, sh_ref, w_ref, xn_ref, pm_ref, pr_ref, *, do_ln):
    x = x_ref[...]
    if do_ln:
        x = _layer_norm(x, g_ref[...], b_ref[...])
    xn_ref[...] = x
    h = x * (1.0 + sc_ref[...]) + sh_ref[...]
    p = jnp.dot(h.astype(BF16), w_ref[...], preferred_element_type=F32)
    pm_ref[...] = p[:, :PM_COLS]
    pr_ref[...] = p[:, PM_COLS:]


def _inproj(x, ln_g, ln_b, sc, sh, w, seq, tm, do_ln):
    t, d = x.shape
    per_b = seq // tm
    n = w.shape[1]
    row = lambda i: (i, 0)
    const = lambda i: (0, 0)
    mod = lambda i: (i // per_b, 0, 0)
    return pl.pallas_call(
        functools.partial(_inproj_kernel, do_ln=do_ln),
        grid=(t // tm,),
        in_specs=[pl.BlockSpec((tm, d), row),
                  pl.BlockSpec((1, d), const), pl.BlockSpec((1, d), const),
                  pl.BlockSpec((None, 1, d), mod), pl.BlockSpec((None, 1, d), mod),
                  pl.BlockSpec((d, n), const)],
        out_specs=[pl.BlockSpec((tm, d), row), pl.BlockSpec((tm, PM_COLS), row),
                   pl.BlockSpec((tm, RWKV_IN), row)],
        out_shape=[jax.ShapeDtypeStruct((t, d), F32), jax.ShapeDtypeStruct((t, PM_COLS), F32),
                   jax.ShapeDtypeStruct((t, RWKV_IN), F32)],
        compiler_params=_cparams(("parallel",)),
        name="inproj",
    )(x, ln_g, ln_b, sc, sh, w)


def _mla_prep_kernel(pm_ref, pos_ref, qg_ref, kvg_ref, wq_ref, wkv_ref, invf_ref, sgn_ref,
                     q_ref, k_ref, v_ref):
    pm = pm_ref[...]
    ang = pos_ref[...].astype(F32) * invf_ref[...]
    cos = jnp.cos(ang)
    sin = jnp.sin(ang) * sgn_ref[...]
    qn = _rms_norm(pm[:, :Q_LORA], qg_ref[...])
    q2 = jnp.dot(qn.astype(BF16), wq_ref[...], preferred_element_type=F32)
    kvn = _rms_norm(pm[:, Q_LORA:Q_LORA + KV_LORA], kvg_ref[...])
    kv = jnp.dot(kvn.astype(BF16), wkv_ref[...], preferred_element_type=F32)
    off = Q_LORA + KV_LORA
    k_rot = pm[:, off:off + LANES] * cos + pm[:, off + LANES:off + 2 * LANES] * sin
    scale = QK_HEAD ** -0.5
    hw = MLA_HEADS * LANES
    ones_hi = (lax.broadcasted_iota(I32, (1, LANES), 1) >= V_HEAD).astype(F32)
    for h in range(MLA_HEADS):
        sl = slice(h * LANES, (h + 1) * LANES)
        sl2 = slice(hw + h * LANES, hw + (h + 1) * LANES)
        q_ref[:, sl] = ((q2[:, sl] * cos + q2[:, sl2] * sin) * scale).astype(BF16)
        k_ref[:, sl] = (kv[:, sl] + k_rot).astype(BF16)
        v_ref[:, sl] = (kv[:, sl2] + ones_hi).astype(BF16)


def _mla_prep(pm, pos, q_g, kv_g, wq2, wkv, invf, sgn, tm):
    t = pm.shape[0]
    hw = MLA_HEADS * LANES
    row = lambda i: (i, 0)
    const = lambda i: (0, 0)
    return pl.pallas_call(
        _mla_prep_kernel,
        grid=(t // tm,),
        in_specs=[pl.BlockSpec((tm, PM_COLS), row), pl.BlockSpec((tm, 1), row),
                  pl.BlockSpec((1, Q_LORA), const), pl.BlockSpec((1, KV_LORA), const),
                  pl.BlockSpec(wq2.shape, const), pl.BlockSpec(wkv.shape, const),
                  pl.BlockSpec((1, LANES), const), pl.BlockSpec((1, LANES), const)],
        out_specs=[pl.BlockSpec((tm, hw), row)] * 3,
        out_shape=[jax.ShapeDtypeStruct((t, hw), BF16)] * 3,
        compiler_params=_cparams(("parallel",)),
        name="mla_prep",
    )(pm, pos, q_g, kv_g, wq2, wkv, invf, sgn)


def _attn_kernel(q_ref, k_ref, v_ref, o_ref, *, tq):
    qi = pl.program_id(2)
    lane = lax.broadcasted_iota(I32, (tq, LANES), 1)
    rows = lax.broadcasted_iota(I32, (tq, tq), 0)
    cols = lax.broadcasted_iota(I32, (tq, tq), 1)
    causal = cols <= rows
    heads = range(2)
    qs = [q_ref[:, h * LANES:(h + 1) * LANES] for h in heads]

    def block(j, carry, masked):
        ms, accs = carry
        start = pl.multiple_of(j * tq, tq)
        ks = [k_ref[pl.ds(start, tq), h * LANES:(h + 1) * LANES] for h in heads]
        vs = [v_ref[pl.ds(start, tq), h * LANES:(h + 1) * LANES] for h in heads]
        ss = [lax.dot_general(qs[h], ks[h], (((1,), (1,)), ((), ())), preferred_element_type=F32)
              for h in heads]
        if masked:
            ss = [jnp.where(causal, s, -jnp.inf) for s in ss]
        m_new = [jnp.maximum(ms[h], jnp.max(ss[h], axis=-1, keepdims=True)) for h in heads]
        alpha = [jnp.exp(ms[h] - m_new[h]) for h in heads]
        ps = [jnp.exp(ss[h] - m_new[h]).astype(BF16) for h in heads]
        acc_new = [alpha[h] * accs[h] + jnp.dot(ps[h], vs[h], preferred_element_type=F32) for h in heads]
        return tuple(m_new), tuple(acc_new)

    init = (tuple(jnp.full((tq, 1), -jnp.inf, F32) for _ in heads),
            tuple(jnp.zeros((tq, LANES), F32) for _ in heads))
    carry = lax.fori_loop(0, qi, functools.partial(block, masked=False), init)
    _, accs = block(qi, carry, True)
    swapped = [pltpu.roll(a, V_HEAD, 1) for a in accs]
    o_ref[...] = jnp.where(lane < V_HEAD, accs[0] / swapped[0], swapped[1] / accs[1])


def _attention(q, k, v, batch, seq, tq):
    t = q.shape[0]
    nq = seq // tq
    return pl.pallas_call(
        functools.partial(_attn_kernel, tq=tq),
        grid=(batch, MLA_HEADS // 2, nq),
        in_specs=[pl.BlockSpec((tq, 2 * LANES), lambda b, h, i: (b * nq + i, h)),
                  pl.BlockSpec((seq, 2 * LANES), lambda b, h, i: (b, h)),
                  pl.BlockSpec((seq, 2 * LANES), lambda b, h, i: (b, h))],
        out_specs=pl.BlockSpec((tq, LANES), lambda b, h, i: (b * nq + i, h)),
        out_shape=jax.ShapeDtypeStruct((t, MLA_OUT), F32),
        compiler_params=_cparams(("parallel", "parallel", "arbitrary")),
        name="attention",
    )(q, k, v)


def _head_ones(n):
    r = lax.broadcasted_iota(I32, (n, n), 0) // RWKV_N
    c = lax.broadcasted_iota(I32, (n, n), 1) // RWKV_N
    return (r == c).astype(BF16)


def _rwkv_prep_kernel(*refs, has_vres):
    if has_vres:
        (p_ref, prev_ref, mu_ref, w0_ref, w2_ref, a0_ref, a2_ref, g2_ref, kk_ref, ka_ref, rk_ref,
         vf_ref, v0_ref, v1_ref, v2_ref,
         r_out, lw_out, k_out, v_out, al_out, be_out, g_out, bonus_out) = refs
    else:
        (p_ref, prev_ref, mu_ref, w0_ref, w2_ref, a0_ref, a2_ref, g2_ref, kk_ref, ka_ref, rk_ref,
         r_out, lw_out, k_out, v_out, al_out, be_out, g_out, bonus_out) = refs
    i = pl.program_id(1)
    p = p_ref[...]
    tm = p.shape[0]
    prev_row = jnp.where(i == 0, 0.0, prev_ref[SUBLANES - 1:SUBLANES, :])
    row = lax.broadcasted_iota(I32, p.shape, 0)
    p_prev = jnp.where(row == 0, prev_row, pltpu.roll(p, 1, 0))
    p = p + mu_ref[...] * (p_prev - p)
    c = RWKV_DIM
    r = p[:, :c]
    k = p[:, c:2 * c]
    v = p[:, 2 * c:3 * c]
    wd = p[:, 3 * c:3 * c + DECAY_LORA]
    ad = p[:, 3 * c + DECAY_LORA:3 * c + DECAY_LORA + ICLR_LORA]
    gd = p[:, 3 * c + DECAY_LORA + ICLR_LORA:]
    z = w0_ref[...] + _bdot(jnp.tanh(wd), w2_ref[...])
    y = -z
    softplus = jnp.maximum(y, 0.0) + jnp.log(1.0 + jnp.exp(-jnp.abs(y)))
    lw_out[...] = -jnp.exp(-softplus - 0.5)
    a = jax.nn.sigmoid(a0_ref[...] + _bdot(ad, a2_ref[...]))
    g_out[...] = _bdot(jax.nn.sigmoid(gd), g2_ref[...])
    if has_vres:
        mix = jax.nn.sigmoid(v0_ref[...] + _bdot(_bdot(v, v1_ref[...]), v2_ref[...]))
        v = v + (vf_ref[...] - v) * mix
    ones = _head_ones(c)
    kk = k * kk_ref[...]
    norm = jnp.sqrt(_split_dot(kk * kk, ones))
    kk = kk / jnp.maximum(norm, 1e-12)
    k = k * (1.0 + (a - 1.0) * ka_ref[...])
    r_out[...] = r
    k_out[...] = k
    v_out[...] = v
    al_out[...] = -kk
    be_out[...] = kk * a
    bonus_out[...] = _split_dot(r * k * rk_ref[...], ones) * v


def _rwkv_prep(pr, mu, w0, w2, a0, a2, g2, k_k, k_a, r_k, vres, batch, seq, tm):
    t = pr.shape[0]
    per_b = seq // tm
    c = RWKV_DIM
    row = lambda b, i: (b * per_b + i, 0)
    const = lambda b, i: (0, 0)
    prev = lambda b, i: (jnp.maximum((b * per_b + i) * (tm // SUBLANES) - 1, 0), 0)
    vec = lambda n: pl.BlockSpec((1, n), const)
    in_specs = [pl.BlockSpec((tm, RWKV_IN), row), pl.BlockSpec((SUBLANES, RWKV_IN), prev),
                vec(RWKV_IN), vec(c), pl.BlockSpec(w2.shape, const), vec(c),
                pl.BlockSpec(a2.shape, const), pl.BlockSpec(g2.shape, const), vec(c), vec(c), vec(c)]
    args = [pr, pr, mu, w0, w2, a0, a2, g2, k_k, k_a, r_k]
    if vres is not None:
        v_first, v0, v1, v2 = vres
        in_specs += [pl.BlockSpec((tm, c), row), vec(c), pl.BlockSpec(v1.shape, const),
                     pl.BlockSpec(v2.shape, const)]
        args += [v_first, v0, v1, v2]
    out = jax.ShapeDtypeStruct((t, c), F32)
    return pl.pallas_call(
        functools.partial(_rwkv_prep_kernel, has_vres=vres is not None),
        grid=(batch, per_b),
        in_specs=in_specs,
        out_specs=[pl.BlockSpec((tm, c), row)] * 8,
        out_shape=[out] * 8,
        compiler_params=_cparams(("parallel", "parallel")),
        name="rwkv_prep",
    )(*args)


def _rwkv_rec_kernel(r_ref, lw_ref, k_ref, v_ref, al_ref, be_ref, o_ref, st_ref, *, nb):
    cs = CHUNK
    gw = RWKV_GROUP * RWKV_N
    ng = RWKV_DIM // gw

    @pl.when(pl.program_id(1) == 0)
    def _():
        st_ref[...] = jnp.zeros_like(st_ref)

    row = lax.broadcasted_iota(I32, (cs, gw), 0)
    lane = lax.broadcasted_iota(I32, (cs, gw), 1)
    col = lane % cs
    head = lane // RWKV_N
    strict = col < row
    incl = col <= row
    eye_g = (col == row).astype(F32)
    r2 = lax.broadcasted_iota(I32, (gw, gw), 0)
    l2 = lax.broadcasted_iota(I32, (gw, gw), 1)
    same_head = (r2 // RWKV_N) == (l2 // RWKV_N)
    eye_sq = r2 == l2
    tri = (lax.broadcasted_iota(I32, (cs, cs), 1) <= lax.broadcasted_iota(I32, (cs, cs), 0)).astype(BF16)

    def bd(x):
        return jnp.concatenate([jnp.where(head == h, x, 0.0) for h in range(RWKV_GROUP)],
                               axis=0).astype(BF16)

    streams = [(b, g) for b in range(nb) for g in range(ng)]
    ld = lambda ref: [ref[b, :, g * gw:(g + 1) * gw] for b, g in streams]
    r, lw, k, v, al, be = ld(r_ref), ld(lw_ref), ld(k_ref), ld(v_ref), ld(al_ref), ld(be_ref)
    each = lambda f, *ls: [f(*xs) for xs in zip(*ls)]

    cum = each(lambda x: _split_dot_left(tri, x), lw)
    cum_end = each(lambda c_: c_[cs - 1:cs, :], cum)
    e_neg = each(lambda c_: jnp.exp(-c_), cum)
    e_end = each(lambda c_, ce: jnp.exp(ce - c_), cum, cum_end)
    a_bar = each(lambda a_, c_, w_: a_ * jnp.exp(c_ - w_), al, cum, lw)
    r_bar = each(lambda r_, c_: r_ * jnp.exp(c_), r, cum)
    b_bar = each(lambda x, e: x * e, be, e_neg)
    k_bar = each(lambda x, e: x * e, k, e_neg)
    b_til = each(lambda x, e: x * e, be, e_end)
    k_til = each(lambda x, e: x * e, k, e_end)
    gram = each(lambda a_, r_, b_, k_: _bdot_nt(jnp.concatenate([a_, r_], axis=0),
                                                jnp.concatenate([bd(b_), bd(k_)], axis=0)),
                a_bar, r_bar, b_bar, k_bar)
    l_ab = each(lambda g_: jnp.where(strict, g_[:cs, :gw], 0.0), gram)
    l_ak = each(lambda g_: jnp.where(strict, g_[:cs, gw:], 0.0), gram)
    m_rb = each(lambda g_: jnp.where(incl, g_[cs:, :gw], 0.0), gram)
    m_rk = each(lambda g_: jnp.where(incl, g_[cs:, gw:], 0.0), gram)
    x = each(lambda l_: eye_g + l_, l_ab)
    lp = l_ab
    for _ in range(5):
        lp = each(lambda l_: _bdot(l_, bd(l_)), lp)
        x = each(lambda x_, l_: x_ + _bdot(x_, bd(l_)), x, lp)
    bdv = each(bd, v)
    lv = each(_bdot, l_ak, bdv)
    w1 = each(lambda x_, a_: _bdot(x_, bd(a_)), x, a_bar)
    u0 = each(lambda x_, l_: _bdot(x_, bd(l_)), x, lv)
    mv = each(_bdot, m_rk, bdv)
    bk_t = each(lambda b_, k_: jnp.concatenate([b_, k_], axis=0).T.astype(BF16), b_til, k_til)
    pc_col = each(lambda ce: jnp.sum(jnp.where(eye_sq, jnp.exp(ce), 0.0), axis=1, keepdims=True), cum_end)
    sb = [st_ref[s] for s in range(len(streams))]
    ws = each(lambda w_, r_, s_: _bdot(jnp.concatenate([w_, r_], axis=0), s_), w1, r_bar, sb)
    u = each(lambda w_, u_: w_[:cs] + u_, ws, u0)
    out = each(lambda w_, m_, u_, mv_: w_[cs:] + _bdot(m_, bd(u_)) + mv_, ws, m_rb, u, mv)
    upd = each(lambda t_, u_, v_: _bdot(t_, jnp.concatenate([u_, v_], axis=0)), bk_t, u, v)
    for s, (b, g) in enumerate(streams):
        o_ref[b, :, g * gw:(g + 1) * gw] = out[s]
        st_ref[s] = pc_col[s] * sb[s] + jnp.where(same_head, upd[s], 0.0)


def _split_dot_left(ones_bf16, x):
    hi = x.astype(BF16)
    lo = (x - hi.astype(F32)).astype(BF16)
    return (jnp.dot(ones_bf16, hi, preferred_element_type=F32)
            + jnp.dot(ones_bf16, lo, preferred_element_type=F32))


def _rwkv_recurrence(r, lw, k, v, al, be, batch, seq, nb):
    t, c = r.shape
    nc = seq // CHUNK
    gw = RWKV_GROUP * RWKV_N
    blk = pl.BlockSpec((nb, CHUNK, c), lambda b, j: (b, j, 0))
    view = lambda a: a.reshape(batch, seq, c)
    out = pl.pallas_call(
        functools.partial(_rwkv_rec_kernel, nb=nb),
        grid=(batch // nb, nc),
        in_specs=[blk] * 6,
        out_specs=blk,
        out_shape=jax.ShapeDtypeStruct((batch, seq, c), F32),
        scratch_shapes=[pltpu.VMEM((nb * (c // gw), gw, gw), F32)],
        compiler_params=_cparams(("parallel", "arbitrary")),
        name="rwkv_recurrence",
    )(view(r), view(lw), view(k), view(v), view(al), view(be))
    return out.reshape(t, c)


def _outproj_kernel(x_ref, att_ref, rec_ref, bonus_ref, g_ref, og_ref, gng_ref, gnb_ref, wo_ref,
                    gt_ref, lng_ref, lnb_ref, sc_ref, sh_ref, rw_ref, rb_ref,
                    x1_ref, h2_ref, gate_ref, pos_ref, meta_ref, run_ref):
    step = pl.program_id(0)

    @pl.when(step == 0)
    def _():
        run_ref[...] = jnp.zeros_like(run_ref)

    tm = x_ref.shape[0]
    mla = _rms_norm(att_ref[...], og_ref[...])
    ones = _head_ones(RWKV_DIM)
    o = rec_ref[...]
    mean = _split_dot(o, ones) * (1.0 / RWKV_N)
    oc = o - mean
    var = _split_dot(oc * oc, ones) * (1.0 / RWKV_N)
    rw = (oc * lax.rsqrt(var + RWKV_GN_EPS) * gng_ref[...] + gnb_ref[...] + bonus_ref[...]) * g_ref[...]
    mix = (jnp.dot(mla.astype(BF16), wo_ref[:MLA_OUT, :], preferred_element_type=F32)
           + jnp.dot(rw.astype(BF16), wo_ref[MLA_OUT:, :], preferred_element_type=F32))
    x1 = _layer_norm(DEEPNORM_ALPHA * x_ref[...] + (1.0 + gt_ref[...]) * mix, lng_ref[...], lnb_ref[...])
    x1_ref[...] = x1
    h2 = x1 * (1.0 + sc_ref[...]) + sh_ref[...]
    h2_ref[...] = h2.astype(BF16)
    logits = jnp.dot(h2, rw_ref[...], preferred_element_type=F32,
                     precision=lax.Precision.HIGHEST) + rb_ref[...]
    e_iota = lax.broadcasted_iota(I32, logits.shape, 1)
    lane = lax.broadcasted_iota(I32, (tm, LANES), 1)
    work = logits
    vals, hots = [], []
    for kk in range(TOP_K):
        m = jnp.max(work, axis=-1, keepdims=True)
        sel = jnp.min(jnp.where(work == m, e_iota, N_EXPERTS), axis=-1, keepdims=True)
        hot = e_iota == sel
        vals.append(m)
        hots.append(hot)
        work = jnp.where(hot, -jnp.inf, work)
    exps = [jnp.exp(vv - vals[0]) for vv in vals]
    denom = exps[0] + exps[1] + exps[2] + exps[3]
    gate_out = jnp.zeros((tm, LANES), F32)
    for kk in range(TOP_K):
        gate_out = jnp.where(lane == kk, exps[kk] / denom, gate_out)
    any_hot = (hots[0] | hots[1] | hots[2] | hots[3]).astype(BF16)
    tri = (lax.broadcasted_iota(I32, (tm, tm), 1) < lax.broadcasted_iota(I32, (tm, tm), 0)).astype(BF16)
    before = jnp.dot(tri, any_hot, preferred_element_type=F32)
    tile_cnt = jnp.sum(any_hot.astype(F32), axis=0, keepdims=True)
    upper = (lax.broadcasted_iota(I32, (N_EXPERTS, N_EXPERTS), 0)
             < lax.broadcasted_iota(I32, (N_EXPERTS, N_EXPERTS), 1)).astype(BF16)
    tile_excl = _split_dot(jnp.broadcast_to(tile_cnt, (SUBLANES, N_EXPERTS)), upper)[0:1]
    pos_out = jnp.zeros((tm, LANES), I32)
    for kk in range(TOP_K):
        pk = jnp.sum(jnp.where(hots[kk], before + tile_excl, 0.0), axis=-1, keepdims=True).astype(I32)
        pos_out = jnp.where(lane == kk, pk, pos_out)
    gate_ref[...] = gate_out
    pos_ref[...] = pos_out
    mrow = lax.broadcasted_iota(I32, (SUBLANES, N_EXPERTS), 0)
    meta = jnp.where(mrow == 0, tile_cnt, jnp.where(mrow == 1, tile_excl, jnp.where(mrow == 2, run_ref[...], 0.0)))
    meta_ref[...] = meta.astype(I32)
    run_ref[...] = run_ref[...] + tile_cnt


def _outproj(x, att, rec, bonus, g, out_g, gn_g, gn_b, wo, gt, ln_g, ln_b, sc, sh, rw, rb, seq, tm):
    t, d = x.shape
    per_b = seq // tm
    row = lambda i: (i, 0)
    const = lambda i: (0, 0)
    mod = lambda i: (i // per_b, 0, 0)
    vec = lambda n: pl.BlockSpec((1, n), const)
    half = lambda: pl.BlockSpec((tm, RWKV_DIM), row)
    n_tiles = t // tm
    return pl.pallas_call(
        _outproj_kernel,
        grid=(t // tm,),
        in_specs=[pl.BlockSpec((tm, d), row), half(), half(), half(), half(),
                  vec(MLA_OUT), vec(RWKV_DIM), vec(RWKV_DIM), pl.BlockSpec(wo.shape, const),
                  pl.BlockSpec((None, 1, d), mod), vec(d), vec(d),
                  pl.BlockSpec((None, 1, d), mod), pl.BlockSpec((None, 1, d), mod),
                  pl.BlockSpec(rw.shape, const), vec(N_EXPERTS)],
        out_specs=[pl.BlockSpec((tm, d), row), pl.BlockSpec((tm, d), row),
                   pl.BlockSpec((tm, LANES), row), pl.BlockSpec((tm, LANES), row),
                   pl.BlockSpec((None, SUBLANES, N_EXPERTS), lambda i: (i, 0, 0))],
        out_shape=[jax.ShapeDtypeStruct((t, d), F32), jax.ShapeDtypeStruct((t, d), BF16),
                   jax.ShapeDtypeStruct((t, LANES), F32), jax.ShapeDtypeStruct((t, LANES), I32),
                   jax.ShapeDtypeStruct((n_tiles, SUBLANES, N_EXPERTS), I32)],
        scratch_shapes=[pltpu.VMEM((1, N_EXPERTS), F32)],
        compiler_params=_cparams(("arbitrary",)),
        name="outproj_router",
    )(x, att, rec, bonus, g, out_g, gn_g, gn_b, wo, gt, ln_g, ln_b, sc, sh, rw, rb)


def _segment_copies(count, src_row, dst_row, src_ref, dst_ref, sem, max_bits, act):
    for b in range(max_bits):
        size = 1 << b

        @pl.when((count & size) != 0)
        def _():
            off = count & (size - 1)
            s = pl.multiple_of((src_row + off) * SUBLANES, SUBLANES)
            d = pl.multiple_of((dst_row + off) * SUBLANES, SUBLANES)
            act(pltpu.make_async_copy(src_ref.at[pl.ds(s, size * SUBLANES)],
                                      dst_ref.at[pl.ds(d, size * SUBLANES)], sem))


def _start(copy):
    copy.start()


def _wait(copy):
    copy.wait()


def _dispatch_kernel(cnt_ref, src_ref, dst_ref, zcnt_ref, zdst_ref, nused_ref, h_ref, pos_ref, xs_hbm,
                     sorted_ref, zeros_ref, sems, zsem, *, ts, bm):
    tau = pl.program_id(0)
    n_tiles = pl.num_programs(0)
    slot = tau % 2
    n_rows = TOP_K * ts
    seg_bits = n_rows.bit_length()
    pad_bits = (bm - 1).bit_length()

    def drain(sl):
        pltpu.make_async_copy(sorted_ref.at[sl], xs_hbm.at[pl.ds(0, n_rows * SUBLANES)], sems.at[sl]).wait()

    @pl.when(tau >= 2)
    def _():
        drain(slot)

    pos_t = pos_ref[...].astype(F32).T
    a = lax.broadcasted_iota(I32, (n_rows, ts), 0).astype(F32)
    perm = (pos_t[0:1] == a) | (pos_t[1:2] == a) | (pos_t[2:3] == a) | (pos_t[3:4] == a)
    srt = jnp.dot(perm.astype(BF16), h_ref[...], preferred_element_type=F32)
    for j in range(D_MODEL // LANES):
        sorted_ref[slot, pl.ds(j, n_rows, stride=SUBLANES), :] = srt[:, j * LANES:(j + 1) * LANES]

    def issue(e, carry):
        i = tau * N_EXPERTS + e
        _segment_copies(cnt_ref[i], src_ref[i], dst_ref[i], sorted_ref.at[slot], xs_hbm, sems.at[slot],
                        seg_bits, _start)
        return carry

    lax.fori_loop(0, N_EXPERTS, issue, 0)

    @pl.when(tau == 0)
    def _():
        zeros_ref[...] = jnp.zeros_like(zeros_ref)

        def pad(act):
            def body(e, carry):
                _segment_copies(zcnt_ref[e], 0, zdst_ref[e], zeros_ref, xs_hbm, zsem, pad_bits, act)
                return carry
            lax.fori_loop(0, N_EXPERTS, body, 0)

        def tail(act):
            def body(blk, carry):
                row0 = pl.multiple_of(blk * (bm * SUBLANES), bm * SUBLANES)
                act(pltpu.make_async_copy(zeros_ref, xs_hbm.at[pl.ds(row0, bm * SUBLANES)], zsem))
                return carry
            lax.fori_loop(nused_ref[0], xs_hbm.shape[0] // (bm * SUBLANES), body, 0)

        pad(_start)
        tail(_start)
        pad(_wait)
        tail(_wait)

    @pl.when(tau == n_tiles - 1)
    def _():
        drain(slot)

    @pl.when((tau == n_tiles - 1) & (tau >= 1))
    def _():
        drain(1 - slot)


def _dispatch(seg_cnt, seg_src, seg_dst, pad_cnt, pad_dst, n_used, h2, pos, n_slots, ts, bm):
    t, d = h2.shape
    n_rows = TOP_K * ts
    grid_spec = pltpu.PrefetchScalarGridSpec(
        num_scalar_prefetch=6,
        grid=(t // ts,),
        in_specs=[pl.BlockSpec((ts, d), lambda i, *_: (i, 0)),
                  pl.BlockSpec((ts, LANES), lambda i, *_: (i, 0))],
        out_specs=pl.BlockSpec(memory_space=pl.ANY),
        scratch_shapes=[pltpu.VMEM((2, n_rows * SUBLANES, LANES), F32),
                        pltpu.VMEM((bm * SUBLANES, LANES), F32),
                        pltpu.SemaphoreType.DMA((2,)), pltpu.SemaphoreType.DMA(())],
    )
    return pl.pallas_call(
        functools.partial(_dispatch_kernel, ts=ts, bm=bm),
        grid_spec=grid_spec,
        out_shape=jax.ShapeDtypeStruct((n_slots * SUBLANES, LANES), F32),
        compiler_params=_cparams(("arbitrary",)),
        name="dispatch",
    )(seg_cnt, seg_src, seg_dst, pad_cnt, pad_dst, n_used, h2, pos)


def _ffn_kernel(be_ref, nused_ref, x_ref, w1_ref, b1_ref, w2_ref, b2_ref, y_ref, *, bm):
    del be_ref
    i = pl.program_id(0)
    f = w2_ref.shape[0]

    @pl.when(i < nused_ref[0])
    def _():
        x = jnp.concatenate([x_ref[pl.ds(j, bm, stride=SUBLANES), :] for j in range(D_MODEL // LANES)],
                            axis=1)
        hh = jnp.dot(x.astype(BF16), w1_ref[...], preferred_element_type=F32) + b1_ref[...]
        x_glu = jnp.minimum(hh[:, :f], SWIGLU_LIMIT)
        x_lin = jnp.clip(hh[:, f:], -SWIGLU_LIMIT, SWIGLU_LIMIT)
        u = x_glu * jax.nn.sigmoid(SWIGLU_ALPHA * x_glu) * (x_lin + 1.0)
        y = jnp.dot(u.astype(BF16), w2_ref[...], preferred_element_type=F32) + b2_ref[...]
        for j in range(D_MODEL // LANES):
            y_ref[pl.ds(j, bm, stride=SUBLANES), :] = y[:, j * LANES:(j + 1) * LANES]

    @pl.when(i >= nused_ref[0])
    def _():
        y_ref[...] = jnp.zeros_like(y_ref)


def _expert_ffn(block_expert, n_used, xs, w1, b1, w2, b2, bm):
    n_blocks = block_expert.shape[0]
    d = w1.shape[1]
    f = w2.shape[1]
    grid_spec = pltpu.PrefetchScalarGridSpec(
        num_scalar_prefetch=2,
        grid=(n_blocks,),
        in_specs=[pl.BlockSpec((bm * SUBLANES, LANES), lambda i, be, nu: (jnp.minimum(i, nu[0] - 1), 0)),
                  pl.BlockSpec((None, d, 2 * f), lambda i, be, nu: (be[i], 0, 0)),
                  pl.BlockSpec((None, 1, 2 * f), lambda i, be, nu: (be[i], 0, 0)),
                  pl.BlockSpec((None, f, d), lambda i, be, nu: (be[i], 0, 0)),
                  pl.BlockSpec((None, 1, d), lambda i, be, nu: (be[i], 0, 0))],
        out_specs=pl.BlockSpec((bm * SUBLANES, LANES), lambda i, be, nu: (i, 0)),
    )
    return pl.pallas_call(
        functools.partial(_ffn_kernel, bm=bm),
        grid_spec=grid_spec,
        out_shape=jax.ShapeDtypeStruct(xs.shape, F32),
        compiler_params=_cparams(("arbitrary",)),
        name="expert_ffn",
    )(block_expert, n_used, xs, w1, b1, w2, b2)


def _combine_kernel(cnt_ref, src_ref, dst_ref, y_hbm, x_ref, pos_ref, gate_ref, gt_ref, lng_ref, lnb_ref,
                    o_ref, ybuf, sems, *, ts):
    tau = pl.program_id(0)
    n_tiles = pl.num_programs(0)
    slot = tau % 2
    n_rows = TOP_K * ts
    seg_bits = n_rows.bit_length()

    def fetch(tile, sl):
        def body(e, carry):
            i = tile * N_EXPERTS + e
            _segment_copies(cnt_ref[i], dst_ref[i], src_ref[i], y_hbm, ybuf.at[sl], sems.at[sl],
                            seg_bits, _start)
            return carry
        lax.fori_loop(0, N_EXPERTS, body, 0)

    @pl.when(tau == 0)
    def _():
        fetch(0, 0)

    @pl.when(tau + 1 < n_tiles)
    def _():
        fetch(tau + 1, 1 - slot)

    pltpu.make_async_copy(y_hbm.at[pl.ds(0, n_rows * SUBLANES)], ybuf.at[slot], sems.at[slot]).wait()
    y = jnp.concatenate([ybuf[slot, pl.ds(j, n_rows, stride=SUBLANES), :] for j in range(D_MODEL // LANES)],
                        axis=1)
    pos = pos_ref[...]
    gates = gate_ref[...]
    a = lax.broadcasted_iota(I32, (ts, n_rows), 1)
    gmat = jnp.zeros((ts, n_rows), F32)
    for kk in range(TOP_K):
        gmat = gmat + jnp.where(pos[:, kk:kk + 1] == a, gates[:, kk:kk + 1], 0.0)
    g_hi = gmat.astype(BF16)
    g_lo = (gmat - g_hi.astype(F32)).astype(BF16)
    y_hi = y.astype(BF16)
    y_lo = (y - y_hi.astype(F32)).astype(BF16)
    ffn = (jnp.dot(g_hi, y_hi, preferred_element_type=F32) + jnp.dot(g_hi, y_lo, preferred_element_type=F32)
           + jnp.dot(g_lo, y_hi, preferred_element_type=F32))
    o_ref[...] = _layer_norm(DEEPNORM_ALPHA * x_ref[...] + (1.0 + gt_ref[...]) * ffn,
                             lng_ref[...], lnb_ref[...])


def _combine(seg_cnt, seg_src, seg_dst, y, x1, pos, gates, gt, ln_g, ln_b, seq, ts):
    t, d = x1.shape
    per_b = seq // ts
    n_rows = TOP_K * ts
    row = lambda i, *_: (i, 0)
    const = lambda i, *_: (0, 0)
    grid_spec = pltpu.PrefetchScalarGridSpec(
        num_scalar_prefetch=3,
        grid=(t // ts,),
        in_specs=[pl.BlockSpec(memory_space=pl.ANY),
                  pl.BlockSpec((ts, d), row), pl.BlockSpec((ts, LANES), row), pl.BlockSpec((ts, LANES), row),
                  pl.BlockSpec((None, 1, d), lambda i, *_: (i // per_b, 0, 0)),
                  pl.BlockSpec((1, d), const), pl.BlockSpec((1, d), const)],
        out_specs=pl.BlockSpec((ts, d), row),
        scratch_shapes=[pltpu.VMEM((2, n_rows * SUBLANES, LANES), F32), pltpu.SemaphoreType.DMA((2,))],
    )
    return pl.pallas_call(
        functools.partial(_combine_kernel, ts=ts),
        grid_spec=grid_spec,
        out_shape=jax.ShapeDtypeStruct((t, d), F32),
        compiler_params=_cparams(("arbitrary",)),
        name="combine",
    )(seg_cnt, seg_src, seg_dst, y, x1, pos, gates, gt, ln_g, ln_b)


def _w1_prep_kernel(w_ref, o_ref):
    f = w_ref.shape[1] // 2
    grp = 2 * LANES
    c = lax.broadcasted_iota(I32, (grp, grp), 0)
    j = lax.broadcasted_iota(I32, (grp, grp), 1)
    perm = (c == jnp.where(j < LANES, 2 * j, 2 * (j - LANES) + 1)).astype(BF16)
    for g in range(w_ref.shape[1] // grp):
        y = jnp.dot(w_ref[:, g * grp:(g + 1) * grp].astype(BF16), perm, preferred_element_type=F32)
        o_ref[:, g * LANES:(g + 1) * LANES] = y[:, :LANES].astype(BF16)
        o_ref[:, f + g * LANES:f + (g + 1) * LANES] = y[:, LANES:].astype(BF16)


def _w1_prep(w1, tk):
    e, d, n = w1.shape
    return pl.pallas_call(
        _w1_prep_kernel,
        grid=(e, d // tk),
        in_specs=[pl.BlockSpec((None, tk, n), lambda i, j: (i, j, 0))],
        out_specs=pl.BlockSpec((None, tk, n), lambda i, j: (i, j, 0)),
        out_shape=jax.ShapeDtypeStruct((e, d, n), BF16),
        compiler_params=_cparams(("parallel", "parallel")),
        name="w1_prep",
    )(w1)


def _rope_cols():
    half = QK_ROPE // 2
    return np.arange(half) * 2, np.arange(half) * 2 + 1


def _build_w_in(w_in):
    d = w_in.shape[0]
    even, odd = _rope_cols()
    kr = w_in[:, Q_LORA + KV_LORA:MLA_IN]
    zeros = lambda n: jnp.zeros((d, n), w_in.dtype)
    placed = jnp.concatenate([zeros(QK_NOPE), kr[:, even], kr[:, odd], zeros(LANES - QK_HEAD)], axis=1)
    swapped = jnp.concatenate([zeros(QK_NOPE), kr[:, odd], kr[:, even], zeros(LANES - QK_HEAD)], axis=1)
    return jnp.concatenate([w_in[:, :Q_LORA + KV_LORA], placed, swapped, w_in[:, MLA_IN:]],
                           axis=1).astype(BF16)


def _build_w_uq(w_uq):
    even, odd = _rope_cols()
    w = w_uq.reshape(Q_LORA, MLA_HEADS, QK_HEAD)
    nope = w[:, :, :QK_NOPE]
    rot = w[:, :, QK_NOPE:]
    pad = jnp.zeros((Q_LORA, MLA_HEADS, LANES - QK_HEAD), w_uq.dtype)
    zero_nope = jnp.zeros_like(nope)
    placed = jnp.concatenate([nope, rot[:, :, even], rot[:, :, odd], pad], axis=2)
    swapped = jnp.concatenate([zero_nope, rot[:, :, odd], rot[:, :, even], pad], axis=2)
    return jnp.concatenate([placed.reshape(Q_LORA, -1), swapped.reshape(Q_LORA, -1)], axis=1).astype(BF16)


def _build_w_ukv(w_uk, w_uv):
    w = w_uk.reshape(KV_LORA, MLA_HEADS, QK_NOPE)
    pad = jnp.zeros((KV_LORA, MLA_HEADS, LANES - QK_NOPE), w_uk.dtype)
    placed = jnp.concatenate([w, pad], axis=2).reshape(KV_LORA, -1)
    wv = w_uv.reshape(KV_LORA, MLA_HEADS, V_HEAD)
    vpad = jnp.zeros((KV_LORA, MLA_HEADS, LANES - V_HEAD), w_uv.dtype)
    v_placed = jnp.concatenate([wv, vpad], axis=2).reshape(KV_LORA, -1)
    return jnp.concatenate([placed, v_placed], axis=1).astype(BF16)


def _rope_rows():
    half = QK_ROPE // 2
    inv = ROPE_THETA ** (-np.arange(0, QK_ROPE, 2, dtype=np.float32) / QK_ROPE)
    invf = np.zeros((1, LANES), np.float32)
    sgn = np.zeros((1, LANES), np.float32)
    invf[0, QK_NOPE:QK_NOPE + half] = inv
    invf[0, QK_NOPE + half:QK_HEAD] = inv
    sgn[0, QK_NOPE:QK_NOPE + half] = -1.0
    sgn[0, QK_NOPE + half:QK_HEAD] = 1.0
    return jnp.asarray(invf), jnp.asarray(sgn)


def kernel(x, c, positions, emb_ln_g, emb_ln_b, ada_w, ada_b, w_in, q_norm_g, w_uq, kv_norm_g, w_uk, w_uv, mla_out_g, rwkv_mu, rwkv_w0, rwkv_w2, rwkv_a0, rwkv_a2, rwkv_g2, rwkv_k_k, rwkv_k_a, rwkv_r_k, rwkv_gn_g, rwkv_gn_b, vres_v0, vres_v1, vres_v2, w_o, ln1_g, ln1_b, router_w, router_b, exp_w1, exp_b1, exp_w2, exp_b2, ln2_g, ln2_b):
    batch, seq, d = x.shape
    t = batch * seq
    tm = min(512, seq)
    tq = min(512, seq)
    ts = min(256, seq)
    bm = 512
    n_blocks = (t * TOP_K) // bm + N_EXPERTS
    n_slots = n_blocks * bm

    row = lambda a: a.reshape(1, -1)
    mod = _modulation(c, ada_w, ada_b)
    invf, sgn = _rope_rows()
    pos = positions.reshape(t, 1)
    xf = x.reshape(t, d)
    v_first = None
    for i in range(DEPTH):
        sh1, sc1, gt1, sh2, sc2, gt2 = [m.reshape(batch, 1, d) for m in jnp.split(mod[i], 6, axis=-1)]
        xn, pm, pr = _inproj(xf, row(emb_ln_g), row(emb_ln_b), sc1, sh1, _build_w_in(w_in[i]),
                             seq, tm, do_ln=(i == 0))
        q, k, v = _mla_prep(pm, pos, row(q_norm_g[i]), row(kv_norm_g[i]), _build_w_uq(w_uq[i]),
                            _build_w_ukv(w_uk[i], w_uv[i]), invf, sgn, tm)
        att = _attention(q, k, v, batch, seq, tq)
        vres = None if i == 0 else (v_first, row(vres_v0[i - 1]), vres_v1[i - 1].astype(BF16),
                                    vres_v2[i - 1].astype(BF16))
        r_, lw, k_, v_, al, be, g_, bonus = _rwkv_prep(
            pr, row(rwkv_mu[i]), row(rwkv_w0[i]), rwkv_w2[i].astype(BF16), row(rwkv_a0[i]),
            rwkv_a2[i].astype(BF16), rwkv_g2[i].astype(BF16), row(rwkv_k_k[i]), row(rwkv_k_a[i]),
            row(rwkv_r_k[i]), vres, batch, seq, tm)
        if i == 0:
            v_first = v_
        rec = _rwkv_recurrence(r_, lw, k_, v_, al, be, batch, seq, nb=min(4, batch))
        x1, h2, gates, pos_in_tile, meta = _outproj(
            xn, att, rec, bonus, g_, row(mla_out_g[i]), row(rwkv_gn_g[i]), row(rwkv_gn_b[i]),
            w_o[i].astype(BF16), gt1, row(ln1_g[i]), row(ln1_b[i]), sc2, sh2, router_w[i],
            row(router_b[i]), seq, ts)
        tile_cnt, tile_src, tile_before = meta[:, 0, :], meta[:, 1, :], meta[:, 2, :]
        cnt = tile_before[-1] + tile_cnt[-1]
        padded = (cnt + bm - 1) // bm * bm
        pad_ends = jnp.cumsum(padded)
        pad_starts = pad_ends - padded
        seg_cnt = tile_cnt.reshape(-1)
        seg_src = tile_src.reshape(-1)
        seg_dst = (pad_starts[None, :] + tile_before).reshape(-1)
        block_expert = jnp.minimum(
            jnp.searchsorted(pad_ends, jnp.arange(n_blocks, dtype=I32) * bm, side='right'),
            N_EXPERTS - 1).astype(I32)
        n_used = (pad_ends[-1:] // bm).astype(I32)
        xs = _dispatch(seg_cnt, seg_src, seg_dst, padded - cnt, pad_starts + cnt, n_used, h2, pos_in_tile,
                       n_slots, ts, bm)
        b1 = jnp.concatenate([exp_b1[i][:, 0::2], exp_b1[i][:, 1::2]], axis=-1)[:, None, :]
        y = _expert_ffn(block_expert, n_used, xs, _w1_prep(exp_w1[i], tm), b1, exp_w2[i].astype(BF16),
                        exp_b2[i][:, None, :], bm)
        xf = _combine(seg_cnt, seg_src, seg_dst, y, x1, pos_in_tile, gates, gt2, row(ln2_g[i]),
                      row(ln2_b[i]), seq, ts)
    return xf.reshape(batch, seq, d)
```

```python
import functools

import jax
import jax.numpy as jnp
import numpy as np
from jax import lax
from jax.experimental import pallas as pl
from jax.experimental.pallas import tpu as pltpu

F32 = jnp.float32
BF16 = jnp.bfloat16
I32 = jnp.int32

D_MODEL = 1024
DEPTH = 2
LANES = 128
SUBLANES = 8

MLA_HEADS = 8
QK_NOPE = 64
QK_ROPE = 32
QK_HEAD = QK_NOPE + QK_ROPE
V_HEAD = 64
Q_LORA = 256
KV_LORA = 128
ROPE_THETA = 10000.0
MLA_OUT = MLA_HEADS * V_HEAD
ATTN_GROUP = 4

RWKV_HEADS = 8
RWKV_N = 64
RWKV_GROUP = 4
RWKV_DIM = RWKV_HEADS * RWKV_N
DECAY_LORA = 64
ICLR_LORA = 64
GATE_LORA = 128
RWKV_GN_EPS = 64e-5
RWKV_IN = 3 * RWKV_DIM + DECAY_LORA + ICLR_LORA + GATE_LORA
MLA_IN = Q_LORA + KV_LORA + QK_ROPE

N_EXPERTS = 32
TOP_K = 4
SWIGLU_LIMIT = 7.0
SWIGLU_ALPHA = 1.702

DEEPNORM_ALPHA = (2 * DEPTH) ** 0.25
LN_EPS = 1e-5
RMS_EPS = 1e-6

PM_COLS = Q_LORA + KV_LORA + 2 * LANES
CHUNK = 64
SEG_COMMON_BITS = 6
VMEM_LIMIT = 56 * 1024 * 1024


def _cparams(sem):
    return pltpu.CompilerParams(dimension_semantics=sem, vmem_limit_bytes=VMEM_LIMIT)


def _bdot(a, b):
    return jnp.dot(a.astype(BF16), b.astype(BF16), preferred_element_type=F32)


def _bdot_nt(a, b):
    return lax.dot_general(a.astype(BF16), b.astype(BF16), (((1,), (1,)), ((), ())),
                           preferred_element_type=F32)


def _split_dot(x, ones_bf16):
    hi = x.astype(BF16)
    lo = (x - hi.astype(F32)).astype(BF16)
    return (jnp.dot(hi, ones_bf16, preferred_element_type=F32)
            + jnp.dot(lo, ones_bf16, preferred_element_type=F32))


def _layer_norm(x, g, b):
    mu = jnp.mean(x, axis=-1, keepdims=True)
    xc = x - mu
    var = jnp.mean(xc * xc, axis=-1, keepdims=True)
    return xc * lax.rsqrt(var + LN_EPS) * g + b


def _rms_norm(x, g):
    return x * lax.rsqrt(jnp.mean(x * x, axis=-1, keepdims=True) + RMS_EPS) * g


def _mod_kernel(c_ref, w_ref, b_ref, o_ref):
    c = c_ref[...]
    c_act = c * jax.nn.sigmoid(c)
    o_ref[...] = jnp.dot(c_act, w_ref[...], preferred_element_type=F32,
                         precision=lax.Precision.HIGHEST) + b_ref[...]


def _modulation(c, ada_w, ada_b):
    b, d = c.shape
    n = ada_w.shape[-1] // d
    return pl.pallas_call(
        _mod_kernel,
        grid=(DEPTH, n),
        in_specs=[pl.BlockSpec((b, d), lambda l, j: (0, 0)),
                  pl.BlockSpec((None, d, d), lambda l, j: (l, 0, j)),
                  pl.BlockSpec((None, 1, d), lambda l, j: (l, 0, j))],
        out_specs=pl.BlockSpec((None, b, d), lambda l, j: (l, 0, j)),
        out_shape=jax.ShapeDtypeStruct((DEPTH, b, n * d), F32),
        compiler_params=_cparams(("arbitrary", "arbitrary")),
        name="modulation",
    )(c, ada_w, ada_b.reshape(DEPTH, 1, n * d))


def _inproj_kernel(x_ref, g_ref, b_ref, sc_ref, sh_ref, w_ref, xn_ref, pm_ref, pr_ref, *, do_ln):
    x = x_ref[...]
    if do_ln:
        x = _layer_norm(x, g_ref[...], b_ref[...])
    xn_ref[...] = x
    h = x * (1.0 + sc_ref[...]) + sh_ref[...]
    p = jnp.dot(h.astype(BF16), w_ref[...], preferred_element_type=F32)
    pm_ref[...] = p[:, :PM_COLS]
    pr_ref[...] = p[:, PM_COLS:]


def _inproj(x, ln_g, ln_b, sc, sh, w, seq, tm, do_ln):
    t, d = x.shape
    per_b = seq // tm
    n = w.shape[1]
    row = lambda i: (i, 0)
    const = lambda i: (0, 0)
    mod = lambda i: (i // per_b, 0, 0)
    return pl.pallas_call(
        functools.partial(_inproj_kernel, do_ln=do_ln),
        grid=(t // tm,),
        in_specs=[pl.BlockSpec((tm, d), row),
                  pl.BlockSpec((1, d), const), pl.BlockSpec((1, d), const),
                  pl.BlockSpec((None, 1, d), mod), pl.BlockSpec((None, 1, d), mod),
                  pl.BlockSpec((d, n), const)],
        out_specs=[pl.BlockSpec((tm, d), row), pl.BlockSpec((tm, PM_COLS), row),
                   pl.BlockSpec((tm, RWKV_IN), row)],
        out_shape=[jax.ShapeDtypeStruct((t, d), F32), jax.ShapeDtypeStruct((t, PM_COLS), F32),
                   jax.ShapeDtypeStruct((t, RWKV_IN), F32)],
        compiler_params=_cparams(("parallel",)),
        name="inproj",
    )(x, ln_g, ln_b, sc, sh, w)


def _mla_prep_kernel(pm_ref, pos_ref, qg_ref, kvg_ref, wq_ref, wkv_ref, invf_ref, sgn_ref,
                     q_ref, k_ref, v_ref):
    pm = pm_ref[...]
    ang = pos_ref[...].astype(F32) * invf_ref[...]
    cos = jnp.cos(ang)
    sin = jnp.sin(ang) * sgn_ref[...]
    qn = _rms_norm(pm[:, :Q_LORA], qg_ref[...])
    q2 = jnp.dot(qn.astype(BF16), wq_ref[...], preferred_element_type=F32)
    kvn = _rms_norm(pm[:, Q_LORA:Q_LORA + KV_LORA], kvg_ref[...])
    kv = jnp.dot(kvn.astype(BF16), wkv_ref[...], preferred_element_type=F32)
    off = Q_LORA + KV_LORA
    k_rot = pm[:, off:off + LANES] * cos + pm[:, off + LANES:off + 2 * LANES] * sin
    scale = QK_HEAD ** -0.5
    hw = MLA_HEADS * LANES
    ones_hi = (lax.broadcasted_iota(I32, (1, LANES), 1) >= V_HEAD).astype(F32)
    for h in range(MLA_HEADS):
        sl = slice(h * LANES, (h + 1) * LANES)
        sl2 = slice(hw + h * LANES, hw + (h + 1) * LANES)
        q_ref[:, sl] = ((q2[:, sl] * cos + q2[:, sl2] * sin) * scale).astype(BF16)
        k_ref[:, sl] = (kv[:, sl] + k_rot).astype(BF16)
        v_ref[:, sl] = (kv[:, sl2] + ones_hi).astype(BF16)


def _mla_prep(pm, pos, q_g, kv_g, wq2, wkv, invf, sgn, tm):
    t = pm.shape[0]
    hw = MLA_HEADS * LANES
    row = lambda i: (i, 0)
    const = lambda i: (0, 0)
    return pl.pallas_call(
        _mla_prep_kernel,
        grid=(t // tm,),
        in_specs=[pl.BlockSpec((tm, PM_COLS), row), pl.BlockSpec((tm, 1), row),
                  pl.BlockSpec((1, Q_LORA), const), pl.BlockSpec((1, KV_LORA), const),
                  pl.BlockSpec(wq2.shape, const), pl.BlockSpec(wkv.shape, const),
                  pl.BlockSpec((1, LANES), const), pl.BlockSpec((1, LANES), const)],
        out_specs=[pl.BlockSpec((tm, hw), row)] * 3,
        out_shape=[jax.ShapeDtypeStruct((t, hw), BF16)] * 3,
        compiler_params=_cparams(("parallel",)),
        name="mla_prep",
    )(pm, pos, q_g, kv_g, wq2, wkv, invf, sgn)


def _attn_kernel(q_ref, k_ref, v_ref, o_ref, *, tq):
    qi = pl.program_id(2)
    lane = lax.broadcasted_iota(I32, (tq, LANES), 1)
    rows = lax.broadcasted_iota(I32, (tq, tq), 0)
    cols = lax.broadcasted_iota(I32, (tq, tq), 1)
    causal = cols <= rows
    heads = range(ATTN_GROUP)
    qs = [q_ref[:, h * LANES:(h + 1) * LANES] for h in heads]

    def block(j, carry, masked):
        ms, accs = carry
        start = pl.multiple_of(j * tq, tq)
        ks = [k_ref[pl.ds(start, tq), h * LANES:(h + 1) * LANES] for h in heads]
        vs = [v_ref[pl.ds(start, tq), h * LANES:(h + 1) * LANES] for h in heads]
        ss = [lax.dot_general(qs[h], ks[h], (((1,), (1,)), ((), ())), preferred_element_type=F32)
              for h in heads]
        if masked:
            ss = [jnp.where(causal, s, -jnp.inf) for s in ss]
        m_new = [jnp.maximum(ms[h], jnp.max(ss[h], axis=-1, keepdims=True)) for h in heads]
        alpha = [jnp.exp(ms[h] - m_new[h]) for h in heads]
        ps = [jnp.exp(ss[h] - m_new[h]).astype(BF16) for h in heads]
        acc_new = [alpha[h] * accs[h] + jnp.dot(ps[h], vs[h], preferred_element_type=F32) for h in heads]
        return tuple(m_new), tuple(acc_new)

    init = (tuple(jnp.full((tq, 1), -jnp.inf, F32) for _ in heads),
            tuple(jnp.zeros((tq, LANES), F32) for _ in heads))
    carry = lax.fori_loop(0, qi, functools.partial(block, masked=False), init)
    _, accs = block(qi, carry, True)
    swapped = [pltpu.roll(a, V_HEAD, 1) for a in accs]
    for p in range(ATTN_GROUP // 2):
        e, o = 2 * p, 2 * p + 1
        o_ref[:, p * LANES:(p + 1) * LANES] = jnp.where(lane < V_HEAD, accs[e] / swapped[e],
                                                        swapped[o] / accs[o])


def _attention(q, k, v, batch, seq, tq):
    t = q.shape[0]
    nq = seq // tq
    gw = ATTN_GROUP * LANES
    return pl.pallas_call(
        functools.partial(_attn_kernel, tq=tq),
        grid=(batch, MLA_HEADS // ATTN_GROUP, nq),
        in_specs=[pl.BlockSpec((tq, gw), lambda b, h, i: (b * nq + i, h)),
                  pl.BlockSpec((seq, gw), lambda b, h, i: (b, h)),
                  pl.BlockSpec((seq, gw), lambda b, h, i: (b, h))],
        out_specs=pl.BlockSpec((tq, gw // 2), lambda b, h, i: (b * nq + i, h)),
        out_shape=jax.ShapeDtypeStruct((t, MLA_OUT), F32),
        compiler_params=_cparams(("parallel", "parallel", "arbitrary")),
        name="attention",
    )(q, k, v)


def _head_ones(n):
    r = lax.broadcasted_iota(I32, (n, n), 0) // RWKV_N
    c = lax.broadcasted_iota(I32, (n, n), 1) // RWKV_N
    return (r == c).astype(BF16)


def _rwkv_prep_kernel(*refs, has_vres):
    if has_vres:
        (p_ref, prev_ref, mu_ref, w0_ref, w2_ref, a0_ref, a2_ref, g2_ref, kk_ref, ka_ref, rk_ref,
         vf_ref, v0_ref, v1_ref, v2_ref,
         r_out, lw_out, k_out, v_out, al_out, be_out, g_out, bonus_out) = refs
    else:
        (p_ref, prev_ref, mu_ref, w0_ref, w2_ref, a0_ref, a2_ref, g2_ref, kk_ref, ka_ref, rk_ref,
         r_out, lw_out, k_out, v_out, al_out, be_out, g_out, bonus_out) = refs
    i = pl.program_id(1)
    p = p_ref[...]
    tm = p.shape[0]
    prev_row = jnp.where(i == 0, 0.0, prev_ref[SUBLANES - 1:SUBLANES, :])
    row = lax.broadcasted_iota(I32, p.shape, 0)
    p_prev = jnp.where(row == 0, prev_row, pltpu.roll(p, 1, 0))
    p = p + mu_ref[...] * (p_prev - p)
    c = RWKV_DIM
    r = p[:, :c]
    k = p[:, c:2 * c]
    v = p[:, 2 * c:3 * c]
    wd = p[:, 3 * c:3 * c + DECAY_LORA]
    ad = p[:, 3 * c + DECAY_LORA:3 * c + DECAY_LORA + ICLR_LORA]
    gd = p[:, 3 * c + DECAY_LORA + ICLR_LORA:]
    z = w0_ref[...] + _bdot(jnp.tanh(wd), w2_ref[...])
    y = -z
    softplus = jnp.maximum(y, 0.0) + jnp.log(1.0 + jnp.exp(-jnp.abs(y)))
    lw_out[...] = -jnp.exp(-softplus - 0.5)
    a = jax.nn.sigmoid(a0_ref[...] + _bdot(ad, a2_ref[...]))
    g_out[...] = _bdot(jax.nn.sigmoid(gd), g2_ref[...])
    if has_vres:
        mix = jax.nn.sigmoid(v0_ref[...] + _bdot(_bdot(v, v1_ref[...]), v2_ref[...]))
        v = v + (vf_ref[...] - v) * mix
    ones = _head_ones(c)
    kk = k * kk_ref[...]
    norm = jnp.sqrt(_split_dot(kk * kk, ones))
    kk = kk / jnp.maximum(norm, 1e-12)
    k = k * (1.0 + (a - 1.0) * ka_ref[...])
    r_out[...] = r
    k_out[...] = k
    v_out[...] = v
    al_out[...] = -kk
    be_out[...] = kk * a
    bonus_out[...] = _split_dot(r * k * rk_ref[...], ones) * v


def _rwkv_prep(pr, mu, w0, w2, a0, a2, g2, k_k, k_a, r_k, vres, batch, seq, tm):
    t = pr.shape[0]
    per_b = seq // tm
    c = RWKV_DIM
    row = lambda b, i: (b * per_b + i, 0)
    const = lambda b, i: (0, 0)
    prev = lambda b, i: (jnp.maximum((b * per_b + i) * (tm // SUBLANES) - 1, 0), 0)
    vec = lambda n: pl.BlockSpec((1, n), const)
    in_specs = [pl.BlockSpec((tm, RWKV_IN), row), pl.BlockSpec((SUBLANES, RWKV_IN), prev),
                vec(RWKV_IN), vec(c), pl.BlockSpec(w2.shape, const), vec(c),
                pl.BlockSpec(a2.shape, const), pl.BlockSpec(g2.shape, const), vec(c), vec(c), vec(c)]
    args = [pr, pr, mu, w0, w2, a0, a2, g2, k_k, k_a, r_k]
    if vres is not None:
        v_first, v0, v1, v2 = vres
        in_specs += [pl.BlockSpec((tm, c), row), vec(c), pl.BlockSpec(v1.shape, const),
                     pl.BlockSpec(v2.shape, const)]
        args += [v_first, v0, v1, v2]
    out = jax.ShapeDtypeStruct((t, c), F32)
    return pl.pallas_call(
        functools.partial(_rwkv_prep_kernel, has_vres=vres is not None),
        grid=(batch, per_b),
        in_specs=in_specs,
        out_specs=[pl.BlockSpec((tm, c), row)] * 8,
        out_shape=[out] * 8,
        compiler_params=_cparams(("parallel", "parallel")),
        name="rwkv_prep",
    )(*args)


def _rwkv_rec_kernel(r_ref, lw_ref, k_ref, v_ref, al_ref, be_ref, o_ref, st_ref, *, nb):
    cs = CHUNK
    gw = RWKV_GROUP * RWKV_N
    ng = RWKV_DIM // gw

    @pl.when(pl.program_id(1) == 0)
    def _():
        st_ref[...] = jnp.zeros_like(st_ref)

    row = lax.broadcasted_iota(I32, (cs, gw), 0)
    lane = lax.broadcasted_iota(I32, (cs, gw), 1)
    col = lane % cs
    head = lane // RWKV_N
    strict = col < row
    incl = col <= row
    eye_g = (col == row).astype(F32)
    r2 = lax.broadcasted_iota(I32, (gw, gw), 0)
    l2 = lax.broadcasted_iota(I32, (gw, gw), 1)
    same_head = (r2 // RWKV_N) == (l2 // RWKV_N)
    eye_sq = r2 == l2
    tri = (lax.broadcasted_iota(I32, (cs, cs), 1) <= lax.broadcasted_iota(I32, (cs, cs), 0)).astype(BF16)

    def bd(x):
        return jnp.concatenate([jnp.where(head == h, x, 0.0) for h in range(RWKV_GROUP)],
                               axis=0).astype(BF16)

    streams = [(b, g) for b in range(nb) for g in range(ng)]
    ld = lambda ref: [ref[b, :, g * gw:(g + 1) * gw] for b, g in streams]
    r, lw, k, v, al, be = ld(r_ref), ld(lw_ref), ld(k_ref), ld(v_ref), ld(al_ref), ld(be_ref)
    each = lambda f, *ls: [f(*xs) for xs in zip(*ls)]

    cum = each(lambda x: _split_dot_left(tri, x), lw)
    cum_end = each(lambda c_: c_[cs - 1:cs, :], cum)
    e_neg = each(lambda c_: jnp.exp(-c_), cum)
    e_end = each(lambda c_, ce: jnp.exp(ce - c_), cum, cum_end)
    a_bar = each(lambda a_, c_, w_: a_ * jnp.exp(c_ - w_), al, cum, lw)
    r_bar = each(lambda r_, c_: r_ * jnp.exp(c_), r, cum)
    b_bar = each(lambda x, e: x * e, be, e_neg)
    k_bar = each(lambda x, e: x * e, k, e_neg)
    b_til = each(lambda x, e: x * e, be, e_end)
    k_til = each(lambda x, e: x * e, k, e_end)
    gram = each(lambda a_, r_, b_, k_: _bdot_nt(jnp.concatenate([a_, r_], axis=0),
                                                jnp.concatenate([bd(b_), bd(k_)], axis=0)),
                a_bar, r_bar, b_bar, k_bar)
    l_ab = each(lambda g_: jnp.where(strict, g_[:cs, :gw], 0.0), gram)
    l_ak = each(lambda g_: jnp.where(strict, g_[:cs, gw:], 0.0), gram)
    m_rb = each(lambda g_: jnp.where(incl, g_[cs:, :gw], 0.0), gram)
    m_rk = each(lambda g_: jnp.where(incl, g_[cs:, gw:], 0.0), gram)
    x = each(lambda l_: eye_g + l_, l_ab)
    lp = l_ab
    for _ in range(5):
        lp = each(lambda l_: _bdot(l_, bd(l_)), lp)
        x = each(lambda x_, l_: x_ + _bdot(x_, bd(l_)), x, lp)
    bdv = each(bd, v)
    lv = each(_bdot, l_ak, bdv)
    w1 = each(lambda x_, a_: _bdot(x_, bd(a_)), x, a_bar)
    u0 = each(lambda x_, l_: _bdot(x_, bd(l_)), x, lv)
    mv = each(_bdot, m_rk, bdv)
    bk_t = each(lambda b_, k_: jnp.concatenate([b_, k_], axis=0).T.astype(BF16), b_til, k_til)
    pc_col = each(lambda ce: jnp.sum(jnp.where(eye_sq, jnp.exp(ce), 0.0), axis=1, keepdims=True), cum_end)
    sb = [st_ref[s] for s in range(len(streams))]
    ws = each(lambda w_, r_, s_: _bdot(jnp.concatenate([w_, r_], axis=0), s_), w1, r_bar, sb)
    u = each(lambda w_, u_: w_[:cs] + u_, ws, u0)
    out = each(lambda w_, m_, u_, mv_: w_[cs:] + _bdot(m_, bd(u_)) + mv_, ws, m_rb, u, mv)
    upd = each(lambda t_, u_, v_: _bdot(t_, jnp.concatenate([u_, v_], axis=0)), bk_t, u, v)
    for s, (b, g) in enumerate(streams):
        o_ref[b, :, g * gw:(g + 1) * gw] = out[s]
        st_ref[s] = pc_col[s] * sb[s] + jnp.where(same_head, upd[s], 0.0)


def _split_dot_left(ones_bf16, x):
    hi = x.astype(BF16)
    lo = (x - hi.astype(F32)).astype(BF16)
    return (jnp.dot(ones_bf16, hi, preferred_element_type=F32)
            + jnp.dot(ones_bf16, lo, preferred_element_type=F32))


def _rwkv_recurrence(r, lw, k, v, al, be, batch, seq, nb):
    t, c = r.shape
    nc = seq // CHUNK
    gw = RWKV_GROUP * RWKV_N
    blk = pl.BlockSpec((nb, CHUNK, c), lambda b, j: (b, j, 0))
    view = lambda a: a.reshape(batch, seq, c)
    out = pl.pallas_call(
        functools.partial(_rwkv_rec_kernel, nb=nb),
        grid=(batch // nb, nc),
        in_specs=[blk] * 6,
        out_specs=blk,
        out_shape=jax.ShapeDtypeStruct((batch, seq, c), F32),
        scratch_shapes=[pltpu.VMEM((nb * (c // gw), gw, gw), F32)],
        compiler_params=_cparams(("parallel", "arbitrary")),
        name="rwkv_recurrence",
    )(view(r), view(lw), view(k), view(v), view(al), view(be))
    return out.reshape(t, c)


def _outproj_kernel(x_ref, att_ref, rec_ref, bonus_ref, g_ref, og_ref, gng_ref, gnb_ref, wo_ref,
                    gt_ref, lng_ref, lnb_ref, sc_ref, sh_ref, rwh_ref, rwl_ref, rb_ref,
                    x1_ref, h2_ref, pg_ref, pgt_ref, meta_ref, run_ref):
    step = pl.program_id(0)

    @pl.when(step == 0)
    def _():
        run_ref[...] = jnp.zeros_like(run_ref)

    tm = x_ref.shape[0]
    mla = _rms_norm(att_ref[...], og_ref[...])
    ones = _head_ones(RWKV_DIM)
    o = rec_ref[...]
    mean = _split_dot(o, ones) * (1.0 / RWKV_N)
    oc = o - mean
    var = _split_dot(oc * oc, ones) * (1.0 / RWKV_N)
    rw = (oc * lax.rsqrt(var + RWKV_GN_EPS) * gng_ref[...] + gnb_ref[...] + bonus_ref[...]) * g_ref[...]
    mix = (jnp.dot(mla.astype(BF16), wo_ref[:MLA_OUT, :], preferred_element_type=F32)
           + jnp.dot(rw.astype(BF16), wo_ref[MLA_OUT:, :], preferred_element_type=F32))
    x1 = _layer_norm(DEEPNORM_ALPHA * x_ref[...] + (1.0 + gt_ref[...]) * mix, lng_ref[...], lnb_ref[...])
    x1_ref[...] = x1
    h2 = x1 * (1.0 + sc_ref[...]) + sh_ref[...]
    h2_ref[...] = h2.astype(BF16)
    h_hi = h2.astype(BF16)
    h_lo = (h2 - h_hi.astype(F32)).astype(BF16)
    logits = (_bdot_nt(rwh_ref[...], h_hi) + _bdot_nt(rwh_ref[...], h_lo) + _bdot_nt(rwl_ref[...], h_hi)
              + rb_ref[...])
    ne = N_EXPERTS
    e_iota = lax.broadcasted_iota(I32, (ne, tm), 0)
    work = logits
    vals, hots = [], []
    for kk in range(TOP_K):
        m = jnp.max(work, axis=0, keepdims=True)
        sel = jnp.min(jnp.where(work == m, e_iota, ne), axis=0, keepdims=True)
        hot = e_iota == sel
        vals.append(m)
        hots.append(hot)
        work = jnp.where(hot, -jnp.inf, work)
    exps = [jnp.exp(vv - vals[0]) for vv in vals]
    denom = exps[0] + exps[1] + exps[2] + exps[3]
    any_hot = (hots[0] | hots[1] | hots[2] | hots[3]).astype(BF16)
    earlier = (lax.broadcasted_iota(I32, (tm, tm), 0) < lax.broadcasted_iota(I32, (tm, tm), 1)).astype(BF16)
    before = jnp.dot(any_hot, earlier, preferred_element_type=F32)
    cnt_col = jnp.sum(any_hot.astype(F32), axis=1, keepdims=True)
    lower = (lax.broadcasted_iota(I32, (ne, ne), 1) < lax.broadcasted_iota(I32, (ne, ne), 0)).astype(BF16)
    excl_col = _split_dot_left(lower, jnp.broadcast_to(cnt_col, (ne, LANES)))[:, 0:1]
    base = before + excl_col
    rows = [jnp.sum(jnp.where(hots[kk], base, 0.0), axis=0, keepdims=True) for kk in range(TOP_K)]
    rows += [exps[kk] / denom for kk in range(TOP_K)]
    pg_t = jnp.concatenate(rows, axis=0)
    pgt_ref[...] = pg_t
    pad = jnp.zeros((LANES - 2 * TOP_K, tm), F32)
    pg_ref[...] = jnp.concatenate([pg_t, pad], axis=0).T
    eye = lax.broadcasted_iota(I32, (ne, ne), 0) == lax.broadcasted_iota(I32, (ne, ne), 1)
    as_row = lambda col: jnp.sum(jnp.where(eye, col, 0.0), axis=0, keepdims=True)
    tile_cnt = as_row(cnt_col)
    mrow = lax.broadcasted_iota(I32, (SUBLANES, ne), 0)
    meta = jnp.where(mrow == 0, tile_cnt,
                     jnp.where(mrow == 1, as_row(excl_col), jnp.where(mrow == 2, run_ref[...], 0.0)))
    meta_ref[...] = meta.astype(I32)
    run_ref[...] = run_ref[...] + tile_cnt


def _outproj(x, att, rec, bonus, g, out_g, gn_g, gn_b, wo, gt, ln_g, ln_b, sc, sh, rw, rb, seq, tm):
    t, d = x.shape
    rw_t = rw.T
    rw_hi = rw_t.astype(BF16)
    rw_lo = (rw_t - rw_hi.astype(F32)).astype(BF16)
    rb = rb.reshape(N_EXPERTS, 1)
    per_b = seq // tm
    row = lambda i: (i, 0)
    const = lambda i: (0, 0)
    mod = lambda i: (i // per_b, 0, 0)
    vec = lambda n: pl.BlockSpec((1, n), const)
    half = lambda: pl.BlockSpec((tm, RWKV_DIM), row)
    n_tiles = t // tm
    return pl.pallas_call(
        _outproj_kernel,
        grid=(t // tm,),
        in_specs=[pl.BlockSpec((tm, d), row), half(), half(), half(), half(),
                  vec(MLA_OUT), vec(RWKV_DIM), vec(RWKV_DIM), pl.BlockSpec(wo.shape, const),
                  pl.BlockSpec((None, 1, d), mod), vec(d), vec(d),
                  pl.BlockSpec((None, 1, d), mod), pl.BlockSpec((None, 1, d), mod),
                  pl.BlockSpec(rw_hi.shape, const), pl.BlockSpec(rw_lo.shape, const),
                  pl.BlockSpec((N_EXPERTS, 1), const)],
        out_specs=[pl.BlockSpec((tm, d), row), pl.BlockSpec((tm, d), row),
                   pl.BlockSpec((tm, LANES), row),
                   pl.BlockSpec((None, 2 * TOP_K, tm), lambda i: (i, 0, 0)),
                   pl.BlockSpec((None, SUBLANES, N_EXPERTS), lambda i: (i, 0, 0))],
        out_shape=[jax.ShapeDtypeStruct((t, d), F32), jax.ShapeDtypeStruct((t, d), BF16),
                   jax.ShapeDtypeStruct((t, LANES), F32),
                   jax.ShapeDtypeStruct((n_tiles, 2 * TOP_K, tm), F32),
                   jax.ShapeDtypeStruct((n_tiles, SUBLANES, N_EXPERTS), I32)],
        scratch_shapes=[pltpu.VMEM((1, N_EXPERTS), F32)],
        compiler_params=_cparams(("arbitrary",)),
        name="outproj_router",
    )(x, att, rec, bonus, g, out_g, gn_g, gn_b, wo, gt, ln_g, ln_b, sc, sh, rw_hi, rw_lo, rb)


def _segment_copies(count, src_row, dst_row, src_ref, dst_ref, sem, max_bits, act):
    def bits(lo, hi):
        for b in range(lo, hi):
            size = 1 << b

            @pl.when((count & size) != 0)
            def _():
                off = count & (size - 1)
                s = pl.multiple_of((src_row + off) * SUBLANES, SUBLANES)
                d = pl.multiple_of((dst_row + off) * SUBLANES, SUBLANES)
                act(pltpu.make_async_copy(src_ref.at[pl.ds(s, size * SUBLANES)],
                                          dst_ref.at[pl.ds(d, size * SUBLANES)], sem))

    split = min(max_bits, SEG_COMMON_BITS)
    bits(0, split)
    if max_bits > split:
        @pl.when(count >= (1 << split))
        def _():
            bits(split, max_bits)


def _start(copy):
    copy.start()


def _wait(copy):
    copy.wait()


def _dispatch_kernel(cnt_ref, src_ref, dst_ref, zcnt_ref, zdst_ref, nused_ref, h_ref, pos_ref, xs_hbm,
                     sorted_ref, zeros_ref, sems, zsem, *, ts, bm):
    tau = pl.program_id(0)
    n_tiles = pl.num_programs(0)
    slot = tau % 2
    n_rows = TOP_K * ts
    seg_bits = n_rows.bit_length()
    pad_bits = (bm - 1).bit_length()

    def drain(sl):
        pltpu.make_async_copy(sorted_ref.at[sl], xs_hbm.at[pl.ds(0, n_rows * SUBLANES)], sems.at[sl]).wait()

    @pl.when(tau >= 2)
    def _():
        drain(slot)

    pos_t = pos_ref[...]
    a = lax.broadcasted_iota(I32, (n_rows, ts), 0).astype(F32)
    perm = (pos_t[0:1] == a) | (pos_t[1:2] == a) | (pos_t[2:3] == a) | (pos_t[3:4] == a)
    srt = jnp.dot(perm.astype(BF16), h_ref[...], preferred_element_type=F32)
    for j in range(D_MODEL // LANES):
        sorted_ref[slot, pl.ds(j, n_rows, stride=SUBLANES), :] = srt[:, j * LANES:(j + 1) * LANES]

    def issue(e, carry):
        i = tau * N_EXPERTS + e
        _segment_copies(cnt_ref[i], src_ref[i], dst_ref[i], sorted_ref.at[slot], xs_hbm, sems.at[slot],
                        seg_bits, _start)
        return carry

    lax.fori_loop(0, N_EXPERTS, issue, 0)

    @pl.when(tau == 0)
    def _():
        zeros_ref[...] = jnp.zeros_like(zeros_ref)

        def pad(act):
            def body(e, carry):
                _segment_copies(zcnt_ref[e], 0, zdst_ref[e], zeros_ref, xs_hbm, zsem, pad_bits, act)
                return carry
            lax.fori_loop(0, N_EXPERTS, body, 0)

        def tail(act):
            def body(blk, carry):
                row0 = pl.multiple_of(blk * (bm * SUBLANES), bm * SUBLANES)
                act(pltpu.make_async_copy(zeros_ref, xs_hbm.at[pl.ds(row0, bm * SUBLANES)], zsem))
                return carry
            lax.fori_loop(nused_ref[0], xs_hbm.shape[0] // (bm * SUBLANES), body, 0)

        pad(_start)
        tail(_start)
        pad(_wait)
        tail(_wait)

    @pl.when(tau == n_tiles - 1)
    def _():
        drain(slot)

    @pl.when((tau == n_tiles - 1) & (tau >= 1))
    def _():
        drain(1 - slot)


def _dispatch(seg_cnt, seg_src, seg_dst, pad_cnt, pad_dst, n_used, h2, pos, n_slots, ts, bm):
    t, d = h2.shape
    n_rows = TOP_K * ts
    grid_spec = pltpu.PrefetchScalarGridSpec(
        num_scalar_prefetch=6,
        grid=(t // ts,),
        in_specs=[pl.BlockSpec((ts, d), lambda i, *_: (i, 0)),
                  pl.BlockSpec((None, 2 * TOP_K, ts), lambda i, *_: (i, 0, 0))],
        out_specs=pl.BlockSpec(memory_space=pl.ANY),
        scratch_shapes=[pltpu.VMEM((2, n_rows * SUBLANES, LANES), F32),
                        pltpu.VMEM((bm * SUBLANES, LANES), F32),
                        pltpu.SemaphoreType.DMA((2,)), pltpu.SemaphoreType.DMA(())],
    )
    return pl.pallas_call(
        functools.partial(_dispatch_kernel, ts=ts, bm=bm),
        grid_spec=grid_spec,
        out_shape=jax.ShapeDtypeStruct((n_slots * SUBLANES, LANES), F32),
        compiler_params=_cparams(("arbitrary",)),
        name="dispatch",
    )(seg_cnt, seg_src, seg_dst, pad_cnt, pad_dst, n_used, h2, pos)


def _ffn_kernel(be_ref, nused_ref, x_ref, w1_ref, b1_ref, w2_ref, b2_ref, y_ref, *, bm):
    del be_ref
    i = pl.program_id(0)
    f = w2_ref.shape[0]

    @pl.when(i < nused_ref[0])
    def _():
        x = jnp.concatenate([x_ref[pl.ds(j, bm, stride=SUBLANES), :] for j in range(D_MODEL // LANES)],
                            axis=1)
        hh = jnp.dot(x.astype(BF16), w1_ref[...], preferred_element_type=F32) + b1_ref[...]
        x_glu = jnp.minimum(hh[:, :f], SWIGLU_LIMIT)
        x_lin = jnp.clip(hh[:, f:], -SWIGLU_LIMIT, SWIGLU_LIMIT)
        u = x_glu * jax.nn.sigmoid(SWIGLU_ALPHA * x_glu) * (x_lin + 1.0)
        y = jnp.dot(u.astype(BF16), w2_ref[...], preferred_element_type=F32) + b2_ref[...]
        for j in range(D_MODEL // LANES):
            y_ref[pl.ds(j, bm, stride=SUBLANES), :] = y[:, j * LANES:(j + 1) * LANES]

    @pl.when(i >= nused_ref[0])
    def _():
        y_ref[...] = jnp.zeros_like(y_ref)


def _expert_ffn(block_expert, n_used, xs, w1, b1, w2, b2, layer, bm):
    n_blocks = block_expert.shape[0]
    d = w1.shape[1]
    f = w2.shape[2]
    last_used = lambda i, nu: jnp.maximum(jnp.minimum(i, nu[0] - 1), 0)
    grid_spec = pltpu.PrefetchScalarGridSpec(
        num_scalar_prefetch=2,
        grid=(n_blocks,),
        in_specs=[pl.BlockSpec((bm * SUBLANES, LANES), lambda i, be, nu: (last_used(i, nu), 0)),
                  pl.BlockSpec((None, d, 2 * f), lambda i, be, nu: (be[i], 0, 0)),
                  pl.BlockSpec((None, None, 1, 2 * f), lambda i, be, nu: (layer, be[i], 0, 0)),
                  pl.BlockSpec((None, None, f, d), lambda i, be, nu: (layer, be[i], 0, 0)),
                  pl.BlockSpec((None, None, 1, d), lambda i, be, nu: (layer, be[i], 0, 0))],
        out_specs=pl.BlockSpec((bm * SUBLANES, LANES), lambda i, be, nu: (i, 0)),
    )
    return pl.pallas_call(
        functools.partial(_ffn_kernel, bm=bm),
        grid_spec=grid_spec,
        out_shape=jax.ShapeDtypeStruct(xs.shape, F32),
        compiler_params=_cparams(("arbitrary",)),
        name="expert_ffn",
    )(block_expert, n_used, xs, w1, b1, w2, b2)


def _combine_kernel(cnt_ref, src_ref, dst_ref, y_hbm, x_ref, pg_ref, gt_ref, lng_ref, lnb_ref,
                    o_ref, ybuf, sems, *, ts):
    tau = pl.program_id(0)
    n_tiles = pl.num_programs(0)
    slot = tau % 2
    n_rows = TOP_K * ts
    seg_bits = n_rows.bit_length()

    def fetch(tile, sl):
        def body(e, carry):
            i = tile * N_EXPERTS + e
            _segment_copies(cnt_ref[i], dst_ref[i], src_ref[i], y_hbm, ybuf.at[sl], sems.at[sl],
                            seg_bits, _start)
            return carry
        lax.fori_loop(0, N_EXPERTS, body, 0)

    @pl.when(tau == 0)
    def _():
        fetch(0, 0)

    @pl.when(tau + 1 < n_tiles)
    def _():
        fetch(tau + 1, 1 - slot)

    pltpu.make_async_copy(y_hbm.at[pl.ds(0, n_rows * SUBLANES)], ybuf.at[slot], sems.at[slot]).wait()
    y = jnp.concatenate([ybuf[slot, pl.ds(j, n_rows, stride=SUBLANES), :] for j in range(D_MODEL // LANES)],
                        axis=1)
    pg = pg_ref[...]
    a = lax.broadcasted_iota(I32, (ts, n_rows), 1).astype(F32)
    gmat = jnp.zeros((ts, n_rows), F32)
    for kk in range(TOP_K):
        gmat = gmat + jnp.where(pg[:, kk:kk + 1] == a, pg[:, TOP_K + kk:TOP_K + kk + 1], 0.0)
    g_hi = gmat.astype(BF16)
    g_lo = (gmat - g_hi.astype(F32)).astype(BF16)
    y_hi = y.astype(BF16)
    y_lo = (y - y_hi.astype(F32)).astype(BF16)
    ffn = (jnp.dot(g_hi, y_hi, preferred_element_type=F32) + jnp.dot(g_hi, y_lo, preferred_element_type=F32)
           + jnp.dot(g_lo, y_hi, preferred_element_type=F32))
    o_ref[...] = _layer_norm(DEEPNORM_ALPHA * x_ref[...] + (1.0 + gt_ref[...]) * ffn,
                             lng_ref[...], lnb_ref[...])


def _combine(seg_cnt, seg_src, seg_dst, y, x1, pg, gt, ln_g, ln_b, seq, ts):
    t, d = x1.shape
    per_b = seq // ts
    n_rows = TOP_K * ts
    row = lambda i, *_: (i, 0)
    const = lambda i, *_: (0, 0)
    grid_spec = pltpu.PrefetchScalarGridSpec(
        num_scalar_prefetch=3,
        grid=(t // ts,),
        in_specs=[pl.BlockSpec(memory_space=pl.ANY),
                  pl.BlockSpec((ts, d), row), pl.BlockSpec((ts, LANES), row),
                  pl.BlockSpec((None, 1, d), lambda i, *_: (i // per_b, 0, 0)),
                  pl.BlockSpec((1, d), const), pl.BlockSpec((1, d), const)],
        out_specs=pl.BlockSpec((ts, d), row),
        scratch_shapes=[pltpu.VMEM((2, n_rows * SUBLANES, LANES), F32), pltpu.SemaphoreType.DMA((2,))],
    )
    return pl.pallas_call(
        functools.partial(_combine_kernel, ts=ts),
        grid_spec=grid_spec,
        out_shape=jax.ShapeDtypeStruct((t, d), F32),
        compiler_params=_cparams(("arbitrary",)),
        name="combine",
    )(seg_cnt, seg_src, seg_dst, y, x1, pg, gt, ln_g, ln_b)


def _w1_prep_kernel(w_ref, o_ref):
    f = w_ref.shape[1] // 2
    grp = 2 * LANES
    c = lax.broadcasted_iota(I32, (grp, grp), 0)
    j = lax.broadcasted_iota(I32, (grp, grp), 1)
    perm = (c == jnp.where(j < LANES, 2 * j, 2 * (j - LANES) + 1)).astype(BF16)
    for g in range(w_ref.shape[1] // grp):
        y = jnp.dot(w_ref[:, g * grp:(g + 1) * grp].astype(BF16), perm, preferred_element_type=F32)
        o_ref[:, g * LANES:(g + 1) * LANES] = y[:, :LANES].astype(BF16)
        o_ref[:, f + g * LANES:f + (g + 1) * LANES] = y[:, LANES:].astype(BF16)


def _w1_prep(w1, layer, tk):
    _, e, d, n = w1.shape
    return pl.pallas_call(
        _w1_prep_kernel,
        grid=(e, d // tk),
        in_specs=[pl.BlockSpec((None, None, tk, n), lambda i, j: (layer, i, j, 0))],
        out_specs=pl.BlockSpec((None, tk, n), lambda i, j: (i, j, 0)),
        out_shape=jax.ShapeDtypeStruct((e, d, n), BF16),
        compiler_params=_cparams(("parallel", "parallel")),
        name="w1_prep",
    )(w1)


def _rope_cols():
    half = QK_ROPE // 2
    return np.arange(half) * 2, np.arange(half) * 2 + 1


def _build_w_in(w_in):
    d = w_in.shape[0]
    even, odd = _rope_cols()
    kr = w_in[:, Q_LORA + KV_LORA:MLA_IN]
    zeros = lambda n: jnp.zeros((d, n), w_in.dtype)
    placed = jnp.concatenate([zeros(QK_NOPE), kr[:, even], kr[:, odd], zeros(LANES - QK_HEAD)], axis=1)
    swapped = jnp.concatenate([zeros(QK_NOPE), kr[:, odd], kr[:, even], zeros(LANES - QK_HEAD)], axis=1)
    return jnp.concatenate([w_in[:, :Q_LORA + KV_LORA], placed, swapped, w_in[:, MLA_IN:]],
                           axis=1).astype(BF16)


def _build_w_uq(w_uq):
    even, odd = _rope_cols()
    w = w_uq.reshape(Q_LORA, MLA_HEADS, QK_HEAD)
    nope = w[:, :, :QK_NOPE]
    rot = w[:, :, QK_NOPE:]
    pad = jnp.zeros((Q_LORA, MLA_HEADS, LANES - QK_HEAD), w_uq.dtype)
    zero_nope = jnp.zeros_like(nope)
    placed = jnp.concatenate([nope, rot[:, :, even], rot[:, :, odd], pad], axis=2)
    swapped = jnp.concatenate([zero_nope, rot[:, :, odd], rot[:, :, even], pad], axis=2)
    return jnp.concatenate([placed.reshape(Q_LORA, -1), swapped.reshape(Q_LORA, -1)], axis=1).astype(BF16)


def _build_w_ukv(w_uk, w_uv):
    w = w_uk.reshape(KV_LORA, MLA_HEADS, QK_NOPE)
    pad = jnp.zeros((KV_LORA, MLA_HEADS, LANES - QK_NOPE), w_uk.dtype)
    placed = jnp.concatenate([w, pad], axis=2).reshape(KV_LORA, -1)
    wv = w_uv.reshape(KV_LORA, MLA_HEADS, V_HEAD)
    vpad = jnp.zeros((KV_LORA, MLA_HEADS, LANES - V_HEAD), w_uv.dtype)
    v_placed = jnp.concatenate([wv, vpad], axis=2).reshape(KV_LORA, -1)
    return jnp.concatenate([placed, v_placed], axis=1).astype(BF16)


def _rope_rows():
    half = QK_ROPE // 2
    inv = ROPE_THETA ** (-np.arange(0, QK_ROPE, 2, dtype=np.float32) / QK_ROPE)
    invf = np.zeros((1, LANES), np.float32)
    sgn = np.zeros((1, LANES), np.float32)
    invf[0, QK_NOPE:QK_NOPE + half] = inv
    invf[0, QK_NOPE + half:QK_HEAD] = inv
    sgn[0, QK_NOPE:QK_NOPE + half] = -1.0
    sgn[0, QK_NOPE + half:QK_HEAD] = 1.0
    return jnp.asarray(invf), jnp.asarray(sgn)


def kernel(x, c, positions, emb_ln_g, emb_ln_b, ada_w, ada_b, w_in, q_norm_g, w_uq, kv_norm_g, w_uk, w_uv, mla_out_g, rwkv_mu, rwkv_w0, rwkv_w2, rwkv_a0, rwkv_a2, rwkv_g2, rwkv_k_k, rwkv_k_a, rwkv_r_k, rwkv_gn_g, rwkv_gn_b, vres_v0, vres_v1, vres_v2, w_o, ln1_g, ln1_b, router_w, router_b, exp_w1, exp_b1, exp_w2, exp_b2, ln2_g, ln2_b):
    batch, seq, d = x.shape
    t = batch * seq
    tm = min(512, seq)
    tq = min(512, seq)
    ts = min(256, seq)
    bm = 512
    n_blocks = (t * TOP_K) // bm + N_EXPERTS
    n_slots = n_blocks * bm

    row = lambda a: a.reshape(1, -1)
    mod = _modulation(c, ada_w, ada_b)
    invf, sgn = _rope_rows()
    pos = positions.reshape(t, 1)
    xf = x.reshape(t, d)
    w2_all = exp_w2.astype(BF16)
    b1_all = jnp.concatenate([exp_b1[..., 0::2], exp_b1[..., 1::2]], axis=-1)[:, :, None, :]
    b2_all = exp_b2[:, :, None, :]
    v_first = None
    for i in range(DEPTH):
        sh1, sc1, gt1, sh2, sc2, gt2 = [m.reshape(batch, 1, d) for m in jnp.split(mod[i], 6, axis=-1)]
        xn, pm, pr = _inproj(xf, row(emb_ln_g), row(emb_ln_b), sc1, sh1, _build_w_in(w_in[i]),
                             seq, tm, do_ln=(i == 0))
        q, k, v = _mla_prep(pm, pos, row(q_norm_g[i]), row(kv_norm_g[i]), _build_w_uq(w_uq[i]),
                            _build_w_ukv(w_uk[i], w_uv[i]), invf, sgn, tm)
        att = _attention(q, k, v, batch, seq, tq)
        vres = None if i == 0 else (v_first, row(vres_v0[i - 1]), vres_v1[i - 1].astype(BF16),
                                    vres_v2[i - 1].astype(BF16))
        r_, lw, k_, v_, al, be, g_, bonus = _rwkv_prep(
            pr, row(rwkv_mu[i]), row(rwkv_w0[i]), rwkv_w2[i].astype(BF16), row(rwkv_a0[i]),
            rwkv_a2[i].astype(BF16), rwkv_g2[i].astype(BF16), row(rwkv_k_k[i]), row(rwkv_k_a[i]),
            row(rwkv_r_k[i]), vres, batch, seq, tm)
        if i == 0:
            v_first = v_
        rec = _rwkv_recurrence(r_, lw, k_, v_, al, be, batch, seq, nb=min(4, batch))
        x1, h2, pg, pg_t, meta = _outproj(
            xn, att, rec, bonus, g_, row(mla_out_g[i]), row(rwkv_gn_g[i]), row(rwkv_gn_b[i]),
            w_o[i].astype(BF16), gt1, row(ln1_g[i]), row(ln1_b[i]), sc2, sh2, router_w[i],
            row(router_b[i]), seq, ts)
        tile_cnt, tile_src, tile_before = meta[:, 0, :], meta[:, 1, :], meta[:, 2, :]
        cnt = tile_before[-1] + tile_cnt[-1]
        padded = (cnt + bm - 1) // bm * bm
        pad_ends = jnp.cumsum(padded)
        pad_starts = pad_ends - padded
        seg_cnt = tile_cnt.reshape(-1)
        seg_src = tile_src.reshape(-1)
        seg_dst = (pad_starts[None, :] + tile_before).reshape(-1)
        block_start = jnp.arange(n_blocks, dtype=I32) * bm
        block_expert = jnp.minimum(jnp.sum((pad_ends[None, :] <= block_start[:, None]).astype(I32), axis=1),
                                   N_EXPERTS - 1)
        n_used = (pad_ends[-1:] // bm).astype(I32)
        xs = _dispatch(seg_cnt, seg_src, seg_dst, padded - cnt, pad_starts + cnt, n_used, h2, pg_t,
                       n_slots, ts, bm)
        y = _expert_ffn(block_expert, n_used, xs, _w1_prep(exp_w1, i, tm), b1_all, w2_all, b2_all, i, bm)
        xf = _combine(seg_cnt, seg_src, seg_dst, y, x1, pg, gt2, row(ln2_g[i]), row(ln2_b[i]), seq, ts)
    return xf.reshape(batch, seq, d)
```

```python
import functools

import jax
import jax.numpy as jnp
import numpy as np
from jax import lax
from jax.experimental import pallas as pl
from jax.experimental.pallas import tpu as pltpu

F32 = jnp.float32
BF16 = jnp.bfloat16
I32 = jnp.int32

D_MODEL = 1024
DEPTH = 2
LANES = 128
SUBLANES = 8

MLA_HEADS = 8
QK_NOPE = 64
QK_ROPE = 32
QK_HEAD = QK_NOPE + QK_ROPE
V_HEAD = 64
Q_LORA = 256
KV_LORA = 128
ROPE_THETA = 10000.0
MLA_OUT = MLA_HEADS * V_HEAD
ATTN_GROUP = 4

RWKV_HEADS = 8
RWKV_N = 64
RWKV_GROUP = 4
RWKV_DIM = RWKV_HEADS * RWKV_N
DECAY_LORA = 64
ICLR_LORA = 64
GATE_LORA = 128
RWKV_GN_EPS = 64e-5
RWKV_IN = 3 * RWKV_DIM + DECAY_LORA + ICLR_LORA + GATE_LORA
MLA_IN = Q_LORA + KV_LORA + QK_ROPE

N_EXPERTS = 32
TOP_K = 4
SWIGLU_LIMIT = 7.0
SWIGLU_ALPHA = 1.702

DEEPNORM_ALPHA = (2 * DEPTH) ** 0.25
LN_EPS = 1e-5
RMS_EPS = 1e-6

PM_COLS = Q_LORA + KV_LORA + 2 * LANES
CHUNK = 64
SEG_COMMON_BITS = 6
VMEM_LIMIT = 56 * 1024 * 1024


def _cparams(sem):
    return pltpu.CompilerParams(dimension_semantics=sem, vmem_limit_bytes=VMEM_LIMIT)


def _bdot(a, b):
    return jnp.dot(a.astype(BF16), b.astype(BF16), preferred_element_type=F32)


def _bdot_nt(a, b):
    return lax.dot_general(a.astype(BF16), b.astype(BF16), (((1,), (1,)), ((), ())),
                           preferred_element_type=F32)


def _split_dot(x, ones_bf16):
    hi = x.astype(BF16)
    lo = (x - hi.astype(F32)).astype(BF16)
    return (jnp.dot(hi, ones_bf16, preferred_element_type=F32)
            + jnp.dot(lo, ones_bf16, preferred_element_type=F32))


def _layer_norm(x, g, b):
    mu = jnp.mean(x, axis=-1, keepdims=True)
    xc = x - mu
    var = jnp.mean(xc * xc, axis=-1, keepdims=True)
    return xc * lax.rsqrt(var + LN_EPS) * g + b


def _rms_norm(x, g):
    return x * lax.rsqrt(jnp.mean(x * x, axis=-1, keepdims=True) + RMS_EPS) * g


def _mod_kernel(c_ref, w_ref, b_ref, o_ref):
    c = c_ref[...]
    c_act = c * jax.nn.sigmoid(c)
    o_ref[...] = jnp.dot(c_act, w_ref[...], preferred_element_type=F32,
                         precision=lax.Precision.HIGHEST) + b_ref[...]


def _modulation(c, ada_w, ada_b):
    b, d = c.shape
    n = ada_w.shape[-1] // d
    return pl.pallas_call(
        _mod_kernel,
        grid=(DEPTH, n),
        in_specs=[pl.BlockSpec((b, d), lambda l, j: (0, 0)),
                  pl.BlockSpec((None, d, d), lambda l, j: (l, 0, j)),
                  pl.BlockSpec((None, 1, d), lambda l, j: (l, 0, j))],
        out_specs=pl.BlockSpec((None, b, d), lambda l, j: (l, 0, j)),
        out_shape=jax.ShapeDtypeStruct((DEPTH, b, n * d), F32),
        compiler_params=_cparams(("arbitrary", "arbitrary")),
        name="modulation",
    )(c, ada_w, ada_b.reshape(DEPTH, 1, n * d))


def _inproj_kernel(x_ref, g_ref, b_ref, sc_ref, sh_ref, w_ref, xn_ref, pm_ref, pr_ref, *, do_ln):
    x = x_ref[...]
    if do_ln:
        x = _layer_norm(x, g_ref[...], b_ref[...])
    xn_ref[...] = x
    h = x * (1.0 + sc_ref[...]) + sh_ref[...]
    p = jnp.dot(h.astype(BF16), w_ref[...], preferred_element_type=F32)
    pm_ref[...] = p[:, :PM_COLS]
    pr_ref[...] = p[:, PM_COLS:]


def _inproj(x, ln_g, ln_b, sc, sh, w, seq, tm, do_ln):
    t, d = x.shape
    per_b = seq // tm
    n = w.shape[1]
    row = lambda i: (i, 0)
    const = lambda i: (0, 0)
    mod = lambda i: (i // per_b, 0, 0)
    return pl.pallas_call(
        functools.partial(_inproj_kernel, do_ln=do_ln),
        grid=(t // tm,),
        in_specs=[pl.BlockSpec((tm, d), row),
                  pl.BlockSpec((1, d), const), pl.BlockSpec((1, d), const),
                  pl.BlockSpec((None, 1, d), mod), pl.BlockSpec((None, 1, d), mod),
                  pl.BlockSpec((d, n), const)],
        out_specs=[pl.BlockSpec((tm, d), row), pl.BlockSpec((tm, PM_COLS), row),
                   pl.BlockSpec((tm, RWKV_IN), row)],
        out_shape=[jax.ShapeDtypeStruct((t, d), F32), jax.ShapeDtypeStruct((t, PM_COLS), F32),
                   jax.ShapeDtypeStruct((t, RWKV_IN), F32)],
        compiler_params=_cparams(("parallel",)),
        name="inproj",
    )(x, ln_g, ln_b, sc, sh, w)


def _mla_prep_kernel(pm_ref, pos_ref, qg_ref, kvg_ref, wq_ref, wk_ref, wvt_ref, invf_ref, sgn_ref,
                     q_ref, k_ref, vt_ref):
    pm = pm_ref[...]
    ang = pos_ref[...].astype(F32) * invf_ref[...]
    cos = jnp.cos(ang)
    sin = jnp.sin(ang) * sgn_ref[...]
    qn = _rms_norm(pm[:, :Q_LORA], qg_ref[...])
    q2 = jnp.dot(qn.astype(BF16), wq_ref[...], preferred_element_type=F32)
    kvn = _rms_norm(pm[:, Q_LORA:Q_LORA + KV_LORA], kvg_ref[...])
    kv = jnp.dot(kvn.astype(BF16), wk_ref[...], preferred_element_type=F32)
    v_t = jnp.dot(wvt_ref[...], kvn.T.astype(BF16), preferred_element_type=F32)
    ones_hi = (lax.broadcasted_iota(I32, (v_t.shape[0], 1), 0) % LANES >= V_HEAD).astype(F32)
    vt_ref[...] = (v_t + ones_hi).astype(BF16)
    off = Q_LORA + KV_LORA
    k_rot = pm[:, off:off + LANES] * cos + pm[:, off + LANES:off + 2 * LANES] * sin
    scale = QK_HEAD ** -0.5
    hw = MLA_HEADS * LANES
    for h in range(MLA_HEADS):
        sl = slice(h * LANES, (h + 1) * LANES)
        sl2 = slice(hw + h * LANES, hw + (h + 1) * LANES)
        q_ref[:, sl] = ((q2[:, sl] * cos + q2[:, sl2] * sin) * scale).astype(BF16)
        k_ref[:, sl] = (kv[:, sl] + k_rot).astype(BF16)


def _mla_prep(pm, pos, q_g, kv_g, wq2, wk, wvt, invf, sgn, batch, seq, tm):
    t = pm.shape[0]
    hw = MLA_HEADS * LANES
    per_b = seq // tm
    row = lambda i: (i, 0)
    const = lambda i: (0, 0)
    return pl.pallas_call(
        _mla_prep_kernel,
        grid=(t // tm,),
        in_specs=[pl.BlockSpec((tm, PM_COLS), row), pl.BlockSpec((tm, 1), row),
                  pl.BlockSpec((1, Q_LORA), const), pl.BlockSpec((1, KV_LORA), const),
                  pl.BlockSpec(wq2.shape, const), pl.BlockSpec(wk.shape, const),
                  pl.BlockSpec(wvt.shape, const),
                  pl.BlockSpec((1, LANES), const), pl.BlockSpec((1, LANES), const)],
        out_specs=[pl.BlockSpec((tm, hw), row), pl.BlockSpec((tm, hw), row),
                   pl.BlockSpec((None, None, hw, tm), lambda i: (i // per_b, i % per_b, 0, 0))],
        out_shape=[jax.ShapeDtypeStruct((t, hw), BF16), jax.ShapeDtypeStruct((t, hw), BF16),
                   jax.ShapeDtypeStruct((batch, per_b, hw, tm), BF16)],
        compiler_params=_cparams(("parallel",)),
        name="mla_prep",
    )(pm, pos, q_g, kv_g, wq2, wk, wvt, invf, sgn)


def _attn_kernel(q_ref, k_ref, vt_ref, o_ref, *, tq):
    qi = pl.program_id(2)
    key = lax.broadcasted_iota(I32, (tq, tq), 0)
    qry = lax.broadcasted_iota(I32, (tq, tq), 1)
    causal = key <= qry
    heads = range(ATTN_GROUP)
    qs = [q_ref[:, h * LANES:(h + 1) * LANES] for h in heads]

    def block(j, carry, masked):
        ms, accs = carry
        start = pl.multiple_of(j * tq, tq)
        ks = [k_ref[pl.ds(start, tq), h * LANES:(h + 1) * LANES] for h in heads]
        vts = [vt_ref[j, h * LANES:(h + 1) * LANES, :] for h in heads]
        ss = [lax.dot_general(ks[h], qs[h], (((1,), (1,)), ((), ())), preferred_element_type=F32)
              for h in heads]
        if masked:
            ss = [jnp.where(causal, s, -jnp.inf) for s in ss]
        m_new = [jnp.maximum(ms[h], jnp.max(ss[h], axis=0, keepdims=True)) for h in heads]
        alpha = [jnp.exp(ms[h] - m_new[h]) for h in heads]
        ps = [jnp.exp(ss[h] - m_new[h]).astype(BF16) for h in heads]
        acc_new = [alpha[h] * accs[h] + jnp.dot(vts[h], ps[h], preferred_element_type=F32) for h in heads]
        return tuple(m_new), tuple(acc_new)

    init = (tuple(jnp.full((1, tq), -jnp.inf, F32) for _ in heads),
            tuple(jnp.zeros((LANES, tq), F32) for _ in heads))
    carry = lax.fori_loop(0, qi, functools.partial(block, masked=False), init)
    _, accs = block(qi, carry, True)
    outs = [a[:V_HEAD] / a[V_HEAD:V_HEAD + 1] for a in accs]
    for p in range(ATTN_GROUP // 2):
        pair = jnp.concatenate([outs[2 * p], outs[2 * p + 1]], axis=0)
        o_ref[:, p * LANES:(p + 1) * LANES] = pair.T


def _attention(q, k, vt, batch, seq, tq):
    t = q.shape[0]
    nq = seq // tq
    gw = ATTN_GROUP * LANES
    return pl.pallas_call(
        functools.partial(_attn_kernel, tq=tq),
        grid=(batch, MLA_HEADS // ATTN_GROUP, nq),
        in_specs=[pl.BlockSpec((tq, gw), lambda b, h, i: (b * nq + i, h)),
                  pl.BlockSpec((seq, gw), lambda b, h, i: (b, h)),
                  pl.BlockSpec((None, nq, gw, tq), lambda b, h, i: (b, 0, h, 0))],
        out_specs=pl.BlockSpec((tq, gw // 2), lambda b, h, i: (b * nq + i, h)),
        out_shape=jax.ShapeDtypeStruct((t, MLA_OUT), F32),
        compiler_params=_cparams(("parallel", "parallel", "arbitrary")),
        name="attention",
    )(q, k, vt)


def _head_ones(n):
    r = lax.broadcasted_iota(I32, (n, n), 0) // RWKV_N
    c = lax.broadcasted_iota(I32, (n, n), 1) // RWKV_N
    return (r == c).astype(BF16)


def _rwkv_prep_kernel(*refs, has_vres):
    if has_vres:
        (p_ref, prev_ref, mu_ref, w0_ref, w2_ref, a0_ref, a2_ref, g2_ref, kk_ref, ka_ref, rk_ref,
         vf_ref, v0_ref, v1_ref, v2_ref,
         r_out, lw_out, k_out, v_out, al_out, be_out, g_out, bonus_out) = refs
    else:
        (p_ref, prev_ref, mu_ref, w0_ref, w2_ref, a0_ref, a2_ref, g2_ref, kk_ref, ka_ref, rk_ref,
         r_out, lw_out, k_out, v_out, al_out, be_out, g_out, bonus_out) = refs
    i = pl.program_id(1)
    p = p_ref[...]
    tm = p.shape[0]
    prev_row = jnp.where(i == 0, 0.0, prev_ref[SUBLANES - 1:SUBLANES, :])
    row = lax.broadcasted_iota(I32, p.shape, 0)
    p_prev = jnp.where(row == 0, prev_row, pltpu.roll(p, 1, 0))
    p = p + mu_ref[...] * (p_prev - p)
    c = RWKV_DIM
    r = p[:, :c]
    k = p[:, c:2 * c]
    v = p[:, 2 * c:3 * c]
    wd = p[:, 3 * c:3 * c + DECAY_LORA]
    ad = p[:, 3 * c + DECAY_LORA:3 * c + DECAY_LORA + ICLR_LORA]
    gd = p[:, 3 * c + DECAY_LORA + ICLR_LORA:]
    z = w0_ref[...] + _bdot(jnp.tanh(wd), w2_ref[...])
    y = -z
    softplus = jnp.maximum(y, 0.0) + jnp.log(1.0 + jnp.exp(-jnp.abs(y)))
    lw_out[...] = -jnp.exp(-softplus - 0.5)
    a = jax.nn.sigmoid(a0_ref[...] + _bdot(ad, a2_ref[...]))
    g_out[...] = _bdot(jax.nn.sigmoid(gd), g2_ref[...])
    if has_vres:
        mix = jax.nn.sigmoid(v0_ref[...] + _bdot(_bdot(v, v1_ref[...]), v2_ref[...]))
        v = v + (vf_ref[...] - v) * mix
    ones = _head_ones(c)
    kk = k * kk_ref[...]
    norm = jnp.sqrt(_split_dot(kk * kk, ones))
    kk = kk / jnp.maximum(norm, 1e-12)
    k = k * (1.0 + (a - 1.0) * ka_ref[...])
    r_out[...] = r
    k_out[...] = k
    v_out[...] = v
    al_out[...] = -kk
    be_out[...] = kk * a
    bonus_out[...] = _split_dot(r * k * rk_ref[...], ones) * v


def _rwkv_prep(pr, mu, w0, w2, a0, a2, g2, k_k, k_a, r_k, vres, batch, seq, tm):
    t = pr.shape[0]
    per_b = seq // tm
    c = RWKV_DIM
    row = lambda b, i: (b * per_b + i, 0)
    const = lambda b, i: (0, 0)
    prev = lambda b, i: (jnp.maximum((b * per_b + i) * (tm // SUBLANES) - 1, 0), 0)
    vec = lambda n: pl.BlockSpec((1, n), const)
    in_specs = [pl.BlockSpec((tm, RWKV_IN), row), pl.BlockSpec((SUBLANES, RWKV_IN), prev),
                vec(RWKV_IN), vec(c), pl.BlockSpec(w2.shape, const), vec(c),
                pl.BlockSpec(a2.shape, const), pl.BlockSpec(g2.shape, const), vec(c), vec(c), vec(c)]
    args = [pr, pr, mu, w0, w2, a0, a2, g2, k_k, k_a, r_k]
    if vres is not None:
        v_first, v0, v1, v2 = vres
        in_specs += [pl.BlockSpec((tm, c), row), vec(c), pl.BlockSpec(v1.shape, const),
                     pl.BlockSpec(v2.shape, const)]
        args += [v_first, v0, v1, v2]
    out = jax.ShapeDtypeStruct((t, c), F32)
    return pl.pallas_call(
        functools.partial(_rwkv_prep_kernel, has_vres=vres is not None),
        grid=(batch, per_b),
        in_specs=in_specs,
        out_specs=[pl.BlockSpec((tm, c), row)] * 8,
        out_shape=[out] * 8,
        compiler_params=_cparams(("parallel", "parallel")),
        name="rwkv_prep",
    )(*args)


def _rwkv_rec_kernel(r_ref, lw_ref, k_ref, v_ref, al_ref, be_ref, o_ref, st_ref, *, nb):
    cs = CHUNK
    gw = RWKV_GROUP * RWKV_N
    ng = RWKV_DIM // gw

    @pl.when(pl.program_id(1) == 0)
    def _():
        st_ref[...] = jnp.zeros_like(st_ref)

    row = lax.broadcasted_iota(I32, (cs, gw), 0)
    lane = lax.broadcasted_iota(I32, (cs, gw), 1)
    col = lane % cs
    head = lane // RWKV_N
    strict = col < row
    incl = col <= row
    eye_g = (col == row).astype(F32)
    r2 = lax.broadcasted_iota(I32, (gw, gw), 0)
    l2 = lax.broadcasted_iota(I32, (gw, gw), 1)
    same_head = (r2 // RWKV_N) == (l2 // RWKV_N)
    eye_sq = r2 == l2
    tri = (lax.broadcasted_iota(I32, (cs, cs), 1) <= lax.broadcasted_iota(I32, (cs, cs), 0)).astype(BF16)

    def bd(x):
        return jnp.concatenate([jnp.where(head == h, x, 0.0) for h in range(RWKV_GROUP)],
                               axis=0).astype(BF16)

    streams = [(b, g) for b in range(nb) for g in range(ng)]
    ld = lambda ref: [ref[b, :, g * gw:(g + 1) * gw] for b, g in streams]
    r, lw, k, v, al, be = ld(r_ref), ld(lw_ref), ld(k_ref), ld(v_ref), ld(al_ref), ld(be_ref)
    each = lambda f, *ls: [f(*xs) for xs in zip(*ls)]

    cum = each(lambda x: _split_dot_left(tri, x), lw)
    cum_end = each(lambda c_: c_[cs - 1:cs, :], cum)
    e_neg = each(lambda c_: jnp.exp(-c_), cum)
    e_end = each(lambda c_, ce: jnp.exp(ce - c_), cum, cum_end)
    a_bar = each(lambda a_, c_, w_: a_ * jnp.exp(c_ - w_), al, cum, lw)
    r_bar = each(lambda r_, c_: r_ * jnp.exp(c_), r, cum)
    b_bar = each(lambda x, e: x * e, be, e_neg)
    k_bar = each(lambda x, e: x * e, k, e_neg)
    b_til = each(lambda x, e: x * e, be, e_end)
    k_til = each(lambda x, e: x * e, k, e_end)
    gram = each(lambda a_, r_, b_, k_: _bdot_nt(jnp.concatenate([a_, r_], axis=0),
                                                jnp.concatenate([bd(b_), bd(k_)], axis=0)),
                a_bar, r_bar, b_bar, k_bar)
    l_ab = each(lambda g_: jnp.where(strict, g_[:cs, :gw], 0.0), gram)
    l_ak = each(lambda g_: jnp.where(strict, g_[:cs, gw:], 0.0), gram)
    m_rb = each(lambda g_: jnp.where(incl, g_[cs:, :gw], 0.0), gram)
    m_rk = each(lambda g_: jnp.where(incl, g_[cs:, gw:], 0.0), gram)
    x = each(lambda l_: eye_g + l_, l_ab)
    lp = l_ab
    for _ in range(5):
        lp = each(lambda l_: _bdot(l_, bd(l_)), lp)
        x = each(lambda x_, l_: x_ + _bdot(x_, bd(l_)), x, lp)
    bdv = each(bd, v)
    lv = each(_bdot, l_ak, bdv)
    w1 = each(lambda x_, a_: _bdot(x_, bd(a_)), x, a_bar)
    u0 = each(lambda x_, l_: _bdot(x_, bd(l_)), x, lv)
    mv = each(_bdot, m_rk, bdv)
    bk_t = each(lambda b_, k_: jnp.concatenate([b_, k_], axis=0).T.astype(BF16), b_til, k_til)
    pc_col = each(lambda ce: jnp.sum(jnp.where(eye_sq, jnp.exp(ce), 0.0), axis=1, keepdims=True), cum_end)
    sb = [st_ref[s] for s in range(len(streams))]
    ws = each(lambda w_, r_, s_: _bdot(jnp.concatenate([w_, r_], axis=0), s_), w1, r_bar, sb)
    u = each(lambda w_, u_: w_[:cs] + u_, ws, u0)
    out = each(lambda w_, m_, u_, mv_: w_[cs:] + _bdot(m_, bd(u_)) + mv_, ws, m_rb, u, mv)
    upd = each(lambda t_, u_, v_: _bdot(t_, jnp.concatenate([u_, v_], axis=0)), bk_t, u, v)
    for s, (b, g) in enumerate(streams):
        o_ref[b, :, g * gw:(g + 1) * gw] = out[s]
        st_ref[s] = pc_col[s] * sb[s] + jnp.where(same_head, upd[s], 0.0)


def _split_dot_left(ones_bf16, x):
    hi = x.astype(BF16)
    lo = (x - hi.astype(F32)).astype(BF16)
    return (jnp.dot(ones_bf16, hi, preferred_element_type=F32)
            + jnp.dot(ones_bf16, lo, preferred_element_type=F32))


def _rwkv_recurrence(r, lw, k, v, al, be, batch, seq, nb):
    t, c = r.shape
    nc = seq // CHUNK
    gw = RWKV_GROUP * RWKV_N
    blk = pl.BlockSpec((nb, CHUNK, c), lambda b, j: (b, j, 0))
    view = lambda a: a.reshape(batch, seq, c)
    out = pl.pallas_call(
        functools.partial(_rwkv_rec_kernel, nb=nb),
        grid=(batch // nb, nc),
        in_specs=[blk] * 6,
        out_specs=blk,
        out_shape=jax.ShapeDtypeStruct((batch, seq, c), F32),
        scratch_shapes=[pltpu.VMEM((nb * (c // gw), gw, gw), F32)],
        compiler_params=_cparams(("parallel", "arbitrary")),
        name="rwkv_recurrence",
    )(view(r), view(lw), view(k), view(v), view(al), view(be))
    return out.reshape(t, c)


def _outproj_kernel(x_ref, att_ref, rec_ref, bonus_ref, g_ref, og_ref, gng_ref, gnb_ref, wo_ref,
                    gt_ref, lng_ref, lnb_ref, sc_ref, sh_ref, rwh_ref, rwl_ref, rb_ref,
                    x1_ref, h2_ref, pg_ref, pgt_ref, meta_ref, run_ref):
    step = pl.program_id(0)

    @pl.when(step == 0)
    def _():
        run_ref[...] = jnp.zeros_like(run_ref)

    tm = x_ref.shape[0]
    mla = _rms_norm(att_ref[...], og_ref[...])
    ones = _head_ones(RWKV_DIM)
    o = rec_ref[...]
    mean = _split_dot(o, ones) * (1.0 / RWKV_N)
    oc = o - mean
    var = _split_dot(oc * oc, ones) * (1.0 / RWKV_N)
    rw = (oc * lax.rsqrt(var + RWKV_GN_EPS) * gng_ref[...] + gnb_ref[...] + bonus_ref[...]) * g_ref[...]
    mix = (jnp.dot(mla.astype(BF16), wo_ref[:MLA_OUT, :], preferred_element_type=F32)
           + jnp.dot(rw.astype(BF16), wo_ref[MLA_OUT:, :], preferred_element_type=F32))
    x1 = _layer_norm(DEEPNORM_ALPHA * x_ref[...] + (1.0 + gt_ref[...]) * mix, lng_ref[...], lnb_ref[...])
    x1_ref[...] = x1
    h2 = x1 * (1.0 + sc_ref[...]) + sh_ref[...]
    h2_ref[...] = h2.astype(BF16)
    h_hi = h2.astype(BF16)
    h_lo = (h2 - h_hi.astype(F32)).astype(BF16)
    logits = (_bdot_nt(rwh_ref[...], h_hi) + _bdot_nt(rwh_ref[...], h_lo) + _bdot_nt(rwl_ref[...], h_hi)
              + rb_ref[...])
    ne = N_EXPERTS
    e_iota = lax.broadcasted_iota(I32, (ne, tm), 0)
    work = logits
    vals, hots = [], []
    for kk in range(TOP_K):
        m = jnp.max(work, axis=0, keepdims=True)
        sel = jnp.min(jnp.where(work == m, e_iota, ne), axis=0, keepdims=True)
        hot = e_iota == sel
        vals.append(m)
        hots.append(hot)
        work = jnp.where(hot, -jnp.inf, work)
    exps = [jnp.exp(vv - vals[0]) for vv in vals]
    denom = exps[0] + exps[1] + exps[2] + exps[3]
    any_hot = (hots[0] | hots[1] | hots[2] | hots[3]).astype(BF16)
    earlier = (lax.broadcasted_iota(I32, (tm, tm), 0) < lax.broadcasted_iota(I32, (tm, tm), 1)).astype(BF16)
    before = jnp.dot(any_hot, earlier, preferred_element_type=F32)
    cnt_col = jnp.sum(any_hot.astype(F32), axis=1, keepdims=True)
    lower = (lax.broadcasted_iota(I32, (ne, ne), 1) < lax.broadcasted_iota(I32, (ne, ne), 0)).astype(BF16)
    excl_col = _split_dot_left(lower, jnp.broadcast_to(cnt_col, (ne, LANES)))[:, 0:1]
    base = before + excl_col
    rows = [jnp.sum(jnp.where(hots[kk], base, 0.0), axis=0, keepdims=True) for kk in range(TOP_K)]
    rows += [exps[kk] / denom for kk in range(TOP_K)]
    pg_t = jnp.concatenate(rows, axis=0)
    pgt_ref[...] = pg_t
    pad = jnp.zeros((LANES - 2 * TOP_K, tm), F32)
    pg_ref[...] = jnp.concatenate([pg_t, pad], axis=0).T
    eye = lax.broadcasted_iota(I32, (ne, ne), 0) == lax.broadcasted_iota(I32, (ne, ne), 1)
    as_row = lambda col: jnp.sum(jnp.where(eye, col, 0.0), axis=0, keepdims=True)
    tile_cnt = as_row(cnt_col)
    mrow = lax.broadcasted_iota(I32, (SUBLANES, ne), 0)
    meta = jnp.where(mrow == 0, tile_cnt,
                     jnp.where(mrow == 1, as_row(excl_col), jnp.where(mrow == 2, run_ref[...], 0.0)))
    meta_ref[...] = meta.astype(I32)
    run_ref[...] = run_ref[...] + tile_cnt


def _outproj(x, att, rec, bonus, g, out_g, gn_g, gn_b, wo, gt, ln_g, ln_b, sc, sh, rw, rb, seq, tm):
    t, d = x.shape
    rw_t = rw.T
    rw_hi = rw_t.astype(BF16)
    rw_lo = (rw_t - rw_hi.astype(F32)).astype(BF16)
    rb = rb.reshape(N_EXPERTS, 1)
    per_b = seq // tm
    row = lambda i: (i, 0)
    const = lambda i: (0, 0)
    mod = lambda i: (i // per_b, 0, 0)
    vec = lambda n: pl.BlockSpec((1, n), const)
    half = lambda: pl.BlockSpec((tm, RWKV_DIM), row)
    n_tiles = t // tm
    return pl.pallas_call(
        _outproj_kernel,
        grid=(t // tm,),
        in_specs=[pl.BlockSpec((tm, d), row), half(), half(), half(), half(),
                  vec(MLA_OUT), vec(RWKV_DIM), vec(RWKV_DIM), pl.BlockSpec(wo.shape, const),
                  pl.BlockSpec((None, 1, d), mod), vec(d), vec(d),
                  pl.BlockSpec((None, 1, d), mod), pl.BlockSpec((None, 1, d), mod),
                  pl.BlockSpec(rw_hi.shape, const), pl.BlockSpec(rw_lo.shape, const),
                  pl.BlockSpec((N_EXPERTS, 1), const)],
        out_specs=[pl.BlockSpec((tm, d), row), pl.BlockSpec((tm, d), row),
                   pl.BlockSpec((tm, LANES), row),
                   pl.BlockSpec((None, 2 * TOP_K, tm), lambda i: (i, 0, 0)),
                   pl.BlockSpec((None, SUBLANES, N_EXPERTS), lambda i: (i, 0, 0))],
        out_shape=[jax.ShapeDtypeStruct((t, d), F32), jax.ShapeDtypeStruct((t, d), BF16),
                   jax.ShapeDtypeStruct((t, LANES), F32),
                   jax.ShapeDtypeStruct((n_tiles, 2 * TOP_K, tm), F32),
                   jax.ShapeDtypeStruct((n_tiles, SUBLANES, N_EXPERTS), I32)],
        scratch_shapes=[pltpu.VMEM((1, N_EXPERTS), F32)],
        compiler_params=_cparams(("arbitrary",)),
        name="outproj_router",
    )(x, att, rec, bonus, g, out_g, gn_g, gn_b, wo, gt, ln_g, ln_b, sc, sh, rw_hi, rw_lo, rb)


def _segment_copies(count, src_row, dst_row, src_ref, dst_ref, sem, max_bits, act):
    def bits(lo, hi):
        for b in range(lo, hi):
            size = 1 << b

            @pl.when((count & size) != 0)
            def _():
                off = count & (size - 1)
                s = pl.multiple_of((src_row + off) * SUBLANES, SUBLANES)
                d = pl.multiple_of((dst_row + off) * SUBLANES, SUBLANES)
                act(pltpu.make_async_copy(src_ref.at[pl.ds(s, size * SUBLANES)],
                                          dst_ref.at[pl.ds(d, size * SUBLANES)], sem))

    split = min(max_bits, SEG_COMMON_BITS)
    bits(0, split)
    if max_bits > split:
        @pl.when(count >= (1 << split))
        def _():
            bits(split, max_bits)


def _start(copy):
    copy.start()


def _wait(copy):
    copy.wait()


def _dispatch_kernel(cnt_ref, src_ref, dst_ref, zcnt_ref, zdst_ref, nused_ref, h_ref, pos_ref, xs_hbm,
                     sorted_ref, zeros_ref, sems, zsem, *, ts, bm):
    tau = pl.program_id(0)
    n_tiles = pl.num_programs(0)
    slot = tau % 2
    n_rows = TOP_K * ts
    seg_bits = n_rows.bit_length()
    pad_bits = (bm - 1).bit_length()

    def drain(sl):
        pltpu.make_async_copy(sorted_ref.at[sl], xs_hbm.at[pl.ds(0, n_rows * SUBLANES)], sems.at[sl]).wait()

    @pl.when(tau >= 2)
    def _():
        drain(slot)

    pos_t = pos_ref[...]
    a = lax.broadcasted_iota(I32, (n_rows, ts), 0).astype(F32)
    perm = (pos_t[0:1] == a) | (pos_t[1:2] == a) | (pos_t[2:3] == a) | (pos_t[3:4] == a)
    srt = jnp.dot(perm.astype(BF16), h_ref[...], preferred_element_type=F32)
    for j in range(D_MODEL // LANES):
        sorted_ref[slot, pl.ds(j, n_rows, stride=SUBLANES), :] = srt[:, j * LANES:(j + 1) * LANES]

    def issue(e, carry):
        i = tau * N_EXPERTS + e
        _segment_copies(cnt_ref[i], src_ref[i], dst_ref[i], sorted_ref.at[slot], xs_hbm, sems.at[slot],
                        seg_bits, _start)
        return carry

    lax.fori_loop(0, N_EXPERTS, issue, 0)

    @pl.when(tau == 0)
    def _():
        zeros_ref[...] = jnp.zeros_like(zeros_ref)

        def pad(act):
            def body(e, carry):
                _segment_copies(zcnt_ref[e], 0, zdst_ref[e], zeros_ref, xs_hbm, zsem, pad_bits, act)
                return carry
            lax.fori_loop(0, N_EXPERTS, body, 0)

        def tail(act):
            def body(blk, carry):
                row0 = pl.multiple_of(blk * (bm * SUBLANES), bm * SUBLANES)
                act(pltpu.make_async_copy(zeros_ref, xs_hbm.at[pl.ds(row0, bm * SUBLANES)], zsem))
                return carry
            lax.fori_loop(nused_ref[0], xs_hbm.shape[0] // (bm * SUBLANES), body, 0)

        pad(_start)
        tail(_start)
        pad(_wait)
        tail(_wait)

    @pl.when(tau == n_tiles - 1)
    def _():
        drain(slot)

    @pl.when((tau == n_tiles - 1) & (tau >= 1))
    def _():
        drain(1 - slot)


def _dispatch(seg_cnt, seg_src, seg_dst, pad_cnt, pad_dst, n_used, h2, pos, n_slots, ts, bm):
    t, d = h2.shape
    n_rows = TOP_K * ts
    grid_spec = pltpu.PrefetchScalarGridSpec(
        num_scalar_prefetch=6,
        grid=(t // ts,),
        in_specs=[pl.BlockSpec((ts, d), lambda i, *_: (i, 0)),
                  pl.BlockSpec((None, 2 * TOP_K, ts), lambda i, *_: (i, 0, 0))],
        out_specs=pl.BlockSpec(memory_space=pl.ANY),
        scratch_shapes=[pltpu.VMEM((2, n_rows * SUBLANES, LANES), F32),
                        pltpu.VMEM((bm * SUBLANES, LANES), F32),
                        pltpu.SemaphoreType.DMA((2,)), pltpu.SemaphoreType.DMA(())],
    )
    return pl.pallas_call(
        functools.partial(_dispatch_kernel, ts=ts, bm=bm),
        grid_spec=grid_spec,
        out_shape=jax.ShapeDtypeStruct((n_slots * SUBLANES, LANES), F32),
        compiler_params=_cparams(("arbitrary",)),
        name="dispatch",
    )(seg_cnt, seg_src, seg_dst, pad_cnt, pad_dst, n_used, h2, pos)


def _ffn_kernel(be_ref, nused_ref, x_ref, w1_ref, b1_ref, w2_ref, b2_ref, y_ref, w1s_ref, w2s_ref, *, bm):
    i = pl.program_id(0)
    f = w2_ref.shape[0]
    used = i < nused_ref[0]
    new_expert = (i == 0) | (be_ref[i] != be_ref[jnp.maximum(i - 1, 0)])

    @pl.when(used & new_expert)
    def _():
        grp = 2 * LANES
        c = lax.broadcasted_iota(I32, (grp, grp), 0)
        j = lax.broadcasted_iota(I32, (grp, grp), 1)
        perm = (c == jnp.where(j < LANES, 2 * j, 2 * (j - LANES) + 1)).astype(BF16)
        for g in range(w1_ref.shape[1] // grp):
            y = jnp.dot(w1_ref[:, g * grp:(g + 1) * grp].astype(BF16), perm, preferred_element_type=F32)
            w1s_ref[:, g * LANES:(g + 1) * LANES] = y[:, :LANES].astype(BF16)
            w1s_ref[:, f + g * LANES:f + (g + 1) * LANES] = y[:, LANES:].astype(BF16)
        w2s_ref[...] = w2_ref[...].astype(BF16)

    @pl.when(used)
    def _():
        x = jnp.concatenate([x_ref[pl.ds(j, bm, stride=SUBLANES), :] for j in range(D_MODEL // LANES)],
                            axis=1)
        hh = jnp.dot(x.astype(BF16), w1s_ref[...], preferred_element_type=F32) + b1_ref[...]
        x_glu = jnp.minimum(hh[:, :f], SWIGLU_LIMIT)
        x_lin = jnp.clip(hh[:, f:], -SWIGLU_LIMIT, SWIGLU_LIMIT)
        u = x_glu * jax.nn.sigmoid(SWIGLU_ALPHA * x_glu) * (x_lin + 1.0)
        y = jnp.dot(u.astype(BF16), w2s_ref[...], preferred_element_type=F32) + b2_ref[...]
        for j in range(D_MODEL // LANES):
            y_ref[pl.ds(j, bm, stride=SUBLANES), :] = y[:, j * LANES:(j + 1) * LANES]

    @pl.when(i >= nused_ref[0])
    def _():
        y_ref[...] = jnp.zeros_like(y_ref)


def _expert_ffn(block_expert, n_used, xs, w1, b1, w2, b2, layer, bm):
    n_blocks = block_expert.shape[0]
    d = w1.shape[2]
    f = w2.shape[2]
    last_used = lambda i, nu: jnp.maximum(jnp.minimum(i, nu[0] - 1), 0)
    of_expert = lambda i, be, nu: (layer, be[i], 0, 0)
    grid_spec = pltpu.PrefetchScalarGridSpec(
        num_scalar_prefetch=2,
        grid=(n_blocks,),
        in_specs=[pl.BlockSpec((bm * SUBLANES, LANES), lambda i, be, nu: (last_used(i, nu), 0)),
                  pl.BlockSpec((None, None, d, 2 * f), of_expert),
                  pl.BlockSpec((None, None, 1, 2 * f), of_expert),
                  pl.BlockSpec((None, None, f, d), of_expert),
                  pl.BlockSpec((None, None, 1, d), of_expert)],
        out_specs=pl.BlockSpec((bm * SUBLANES, LANES), lambda i, be, nu: (i, 0)),
        scratch_shapes=[pltpu.VMEM((d, 2 * f), BF16), pltpu.VMEM((f, d), BF16)],
    )
    return pl.pallas_call(
        functools.partial(_ffn_kernel, bm=bm),
        grid_spec=grid_spec,
        out_shape=jax.ShapeDtypeStruct(xs.shape, F32),
        compiler_params=_cparams(("arbitrary",)),
        name="expert_ffn",
    )(block_expert, n_used, xs, w1, b1, w2, b2)


def _combine_kernel(cnt_ref, src_ref, dst_ref, y_hbm, x_ref, pg_ref, gt_ref, lng_ref, lnb_ref,
                    o_ref, ybuf, sems, *, ts):
    tau = pl.program_id(0)
    n_tiles = pl.num_programs(0)
    slot = tau % 2
    n_rows = TOP_K * ts
    seg_bits = n_rows.bit_length()

    def fetch(tile, sl):
        def body(e, carry):
            i = tile * N_EXPERTS + e
            _segment_copies(cnt_ref[i], dst_ref[i], src_ref[i], y_hbm, ybuf.at[sl], sems.at[sl],
                            seg_bits, _start)
            return carry
        lax.fori_loop(0, N_EXPERTS, body, 0)

    @pl.when(tau == 0)
    def _():
        fetch(0, 0)

    @pl.when(tau + 1 < n_tiles)
    def _():
        fetch(tau + 1, 1 - slot)

    pltpu.make_async_copy(y_hbm.at[pl.ds(0, n_rows * SUBLANES)], ybuf.at[slot], sems.at[slot]).wait()
    y = jnp.concatenate([ybuf[slot, pl.ds(j, n_rows, stride=SUBLANES), :] for j in range(D_MODEL // LANES)],
                        axis=1)
    pg = pg_ref[...]
    a = lax.broadcasted_iota(I32, (ts, n_rows), 1).astype(F32)
    gmat = jnp.zeros((ts, n_rows), F32)
    for kk in range(TOP_K):
        gmat = gmat + jnp.where(pg[:, kk:kk + 1] == a, pg[:, TOP_K + kk:TOP_K + kk + 1], 0.0)
    g_hi = gmat.astype(BF16)
    g_lo = (gmat - g_hi.astype(F32)).astype(BF16)
    y_hi = y.astype(BF16)
    y_lo = (y - y_hi.astype(F32)).astype(BF16)
    ffn = (jnp.dot(g_hi, y_hi, preferred_element_type=F32) + jnp.dot(g_hi, y_lo, preferred_element_type=F32)
           + jnp.dot(g_lo, y_hi, preferred_element_type=F32))
    o_ref[...] = _layer_norm(DEEPNORM_ALPHA * x_ref[...] + (1.0 + gt_ref[...]) * ffn,
                             lng_ref[...], lnb_ref[...])


def _combine(seg_cnt, seg_src, seg_dst, y, x1, pg, gt, ln_g, ln_b, seq, ts):
    t, d = x1.shape
    per_b = seq // ts
    n_rows = TOP_K * ts
    row = lambda i, *_: (i, 0)
    const = lambda i, *_: (0, 0)
    grid_spec = pltpu.PrefetchScalarGridSpec(
        num_scalar_prefetch=3,
        grid=(t // ts,),
        in_specs=[pl.BlockSpec(memory_space=pl.ANY),
                  pl.BlockSpec((ts, d), row), pl.BlockSpec((ts, LANES), row),
                  pl.BlockSpec((None, 1, d), lambda i, *_: (i // per_b, 0, 0)),
                  pl.BlockSpec((1, d), const), pl.BlockSpec((1, d), const)],
        out_specs=pl.BlockSpec((ts, d), row),
        scratch_shapes=[pltpu.VMEM((2, n_rows * SUBLANES, LANES), F32), pltpu.SemaphoreType.DMA((2,))],
    )
    return pl.pallas_call(
        functools.partial(_combine_kernel, ts=ts),
        grid_spec=grid_spec,
        out_shape=jax.ShapeDtypeStruct((t, d), F32),
        compiler_params=_cparams(("arbitrary",)),
        name="combine",
    )(seg_cnt, seg_src, seg_dst, y, x1, pg, gt, ln_g, ln_b)


def _rope_cols():
    half = QK_ROPE // 2
    return np.arange(half) * 2, np.arange(half) * 2 + 1


def _build_w_in(w_in):
    d = w_in.shape[0]
    even, odd = _rope_cols()
    kr = w_in[:, Q_LORA + KV_LORA:MLA_IN]
    zeros = lambda n: jnp.zeros((d, n), w_in.dtype)
    placed = jnp.concatenate([zeros(QK_NOPE), kr[:, even], kr[:, odd], zeros(LANES - QK_HEAD)], axis=1)
    swapped = jnp.concatenate([zeros(QK_NOPE), kr[:, odd], kr[:, even], zeros(LANES - QK_HEAD)], axis=1)
    return jnp.concatenate([w_in[:, :Q_LORA + KV_LORA], placed, swapped, w_in[:, MLA_IN:]],
                           axis=1).astype(BF16)


def _build_w_uq(w_uq):
    even, odd = _rope_cols()
    w = w_uq.reshape(Q_LORA, MLA_HEADS, QK_HEAD)
    nope = w[:, :, :QK_NOPE]
    rot = w[:, :, QK_NOPE:]
    pad = jnp.zeros((Q_LORA, MLA_HEADS, LANES - QK_HEAD), w_uq.dtype)
    zero_nope = jnp.zeros_like(nope)
    placed = jnp.concatenate([nope, rot[:, :, even], rot[:, :, odd], pad], axis=2)
    swapped = jnp.concatenate([zero_nope, rot[:, :, odd], rot[:, :, even], pad], axis=2)
    return jnp.concatenate([placed.reshape(Q_LORA, -1), swapped.reshape(Q_LORA, -1)], axis=1).astype(BF16)


def _build_w_ukv(w_uk, w_uv):
    w = w_uk.reshape(KV_LORA, MLA_HEADS, QK_NOPE)
    pad = jnp.zeros((KV_LORA, MLA_HEADS, LANES - QK_NOPE), w_uk.dtype)
    placed = jnp.concatenate([w, pad], axis=2).reshape(KV_LORA, -1)
    wv = w_uv.reshape(KV_LORA, MLA_HEADS, V_HEAD)
    vpad = jnp.zeros((KV_LORA, MLA_HEADS, LANES - V_HEAD), w_uv.dtype)
    v_placed = jnp.concatenate([wv, vpad], axis=2).reshape(KV_LORA, -1)
    return placed.astype(BF16), v_placed.T.astype(BF16)


def _rope_rows():
    half = QK_ROPE // 2
    inv = ROPE_THETA ** (-np.arange(0, QK_ROPE, 2, dtype=np.float32) / QK_ROPE)
    invf = np.zeros((1, LANES), np.float32)
    sgn = np.zeros((1, LANES), np.float32)
    invf[0, QK_NOPE:QK_NOPE + half] = inv
    invf[0, QK_NOPE + half:QK_HEAD] = inv
    sgn[0, QK_NOPE:QK_NOPE + half] = -1.0
    sgn[0, QK_NOPE + half:QK_HEAD] = 1.0
    return jnp.asarray(invf), jnp.asarray(sgn)


def kernel(x, c, positions, emb_ln_g, emb_ln_b, ada_w, ada_b, w_in, q_norm_g, w_uq, kv_norm_g, w_uk, w_uv, mla_out_g, rwkv_mu, rwkv_w0, rwkv_w2, rwkv_a0, rwkv_a2, rwkv_g2, rwkv_k_k, rwkv_k_a, rwkv_r_k, rwkv_gn_g, rwkv_gn_b, vres_v0, vres_v1, vres_v2, w_o, ln1_g, ln1_b, router_w, router_b, exp_w1, exp_b1, exp_w2, exp_b2, ln2_g, ln2_b):
    batch, seq, d = x.shape
    t = batch * seq
    tm = min(512, seq)
    tq = min(512, seq)
    ts = min(256, seq)
    bm = 512
    n_blocks = (t * TOP_K) // bm + N_EXPERTS
    n_slots = n_blocks * bm

    row = lambda a: a.reshape(1, -1)
    mod = _modulation(c, ada_w, ada_b)
    invf, sgn = _rope_rows()
    pos = positions.reshape(t, 1)
    xf = x.reshape(t, d)
    b1_all = jnp.concatenate([exp_b1[..., 0::2], exp_b1[..., 1::2]], axis=-1)[:, :, None, :]
    b2_all = exp_b2[:, :, None, :]
    v_first = None
    for i in range(DEPTH):
        sh1, sc1, gt1, sh2, sc2, gt2 = [m.reshape(batch, 1, d) for m in jnp.split(mod[i], 6, axis=-1)]
        xn, pm, pr = _inproj(xf, row(emb_ln_g), row(emb_ln_b), sc1, sh1, _build_w_in(w_in[i]),
                             seq, tm, do_ln=(i == 0))
        wk, wvt = _build_w_ukv(w_uk[i], w_uv[i])
        q, k, vt = _mla_prep(pm, pos, row(q_norm_g[i]), row(kv_norm_g[i]), _build_w_uq(w_uq[i]),
                             wk, wvt, invf, sgn, batch, seq, tq)
        att = _attention(q, k, vt, batch, seq, tq)
        vres = None if i == 0 else (v_first, row(vres_v0[i - 1]), vres_v1[i - 1].astype(BF16),
                                    vres_v2[i - 1].astype(BF16))
        r_, lw, k_, v_, al, be, g_, bonus = _rwkv_prep(
            pr, row(rwkv_mu[i]), row(rwkv_w0[i]), rwkv_w2[i].astype(BF16), row(rwkv_a0[i]),
            rwkv_a2[i].astype(BF16), rwkv_g2[i].astype(BF16), row(rwkv_k_k[i]), row(rwkv_k_a[i]),
            row(rwkv_r_k[i]), vres, batch, seq, tm)
        if i == 0:
            v_first = v_
        rec = _rwkv_recurrence(r_, lw, k_, v_, al, be, batch, seq, nb=min(4, batch))
        x1, h2, pg, pg_t, meta = _outproj(
            xn, att, rec, bonus, g_, row(mla_out_g[i]), row(rwkv_gn_g[i]), row(rwkv_gn_b[i]),
            w_o[i].astype(BF16), gt1, row(ln1_g[i]), row(ln1_b[i]), sc2, sh2, router_w[i],
            row(router_b[i]), seq, ts)
        tile_cnt, tile_src, tile_before = meta[:, 0, :], meta[:, 1, :], meta[:, 2, :]
        cnt = tile_before[-1] + tile_cnt[-1]
        padded = (cnt + bm - 1) // bm * bm
        pad_ends = jnp.cumsum(padded)
        pad_starts = pad_ends - padded
        seg_cnt = tile_cnt.reshape(-1)
        seg_src = tile_src.reshape(-1)
        seg_dst = (pad_starts[None, :] + tile_before).reshape(-1)
        block_start = jnp.arange(n_blocks, dtype=I32) * bm
        block_expert = jnp.minimum(jnp.sum((pad_ends[None, :] <= block_start[:, None]).astype(I32), axis=1),
                                   N_EXPERTS - 1)
        n_used = (pad_ends[-1:] // bm).astype(I32)
        xs = _dispatch(seg_cnt, seg_src, seg_dst, padded - cnt, pad_starts + cnt, n_used, h2, pg_t,
                       n_slots, ts, bm)
        y = _expert_ffn(block_expert, n_used, xs, exp_w1, b1_all, exp_w2, b2_all, i, bm)
        xf = _combine(seg_cnt, seg_src, seg_dst, y, x1, pg, gt2, row(ln2_g[i]), row(ln2_b[i]), seq, ts)
    return xf.reshape(batch, seq, d)
```

```python
import functools

import jax
import jax.numpy as jnp
import numpy as np
from jax import lax
from jax.experimental import pallas as pl
from jax.experimental.pallas import tpu as pltpu

F32 = jnp.float32
BF16 = jnp.bfloat16
I32 = jnp.int32

D_MODEL = 1024
DEPTH = 2
LANES = 128
SUBLANES = 8

MLA_HEADS = 8
QK_NOPE = 64
QK_ROPE = 32
QK_HEAD = QK_NOPE + QK_ROPE
V_HEAD = 64
Q_LORA = 256
KV_LORA = 128
ROPE_THETA = 10000.0
MLA_OUT = MLA_HEADS * V_HEAD
ATTN_GROUP = 4

RWKV_HEADS = 8
RWKV_N = 64
RWKV_GROUP = 4
RWKV_DIM = RWKV_HEADS * RWKV_N
DECAY_LORA = 64
ICLR_LORA = 64
GATE_LORA = 128
RWKV_GN_EPS = 64e-5
RWKV_IN = 3 * RWKV_DIM + DECAY_LORA + ICLR_LORA + GATE_LORA
MLA_IN = Q_LORA + KV_LORA + QK_ROPE

N_EXPERTS = 32
TOP_K = 4
SWIGLU_LIMIT = 7.0
SWIGLU_ALPHA = 1.702

DEEPNORM_ALPHA = (2 * DEPTH) ** 0.25
LN_EPS = 1e-5
RMS_EPS = 1e-6

PM_COLS = Q_LORA + KV_LORA + 2 * LANES
CHUNK = 64
PREV_ROWS = 16
SEG_COMMON_BITS = 6
VMEM_LIMIT = 56 * 1024 * 1024


def _cparams(sem):
    return pltpu.CompilerParams(dimension_semantics=sem, vmem_limit_bytes=VMEM_LIMIT)


def _bdot(a, b):
    return jnp.dot(a.astype(BF16), b.astype(BF16), preferred_element_type=F32)


def _bdot_nt(a, b):
    return lax.dot_general(a.astype(BF16), b.astype(BF16), (((1,), (1,)), ((), ())),
                           preferred_element_type=F32)


def _split_dot(x, ones_bf16):
    hi = x.astype(BF16)
    lo = (x - hi.astype(F32)).astype(BF16)
    return (jnp.dot(hi, ones_bf16, preferred_element_type=F32)
            + jnp.dot(lo, ones_bf16, preferred_element_type=F32))


def _layer_norm(x, g, b):
    mu = jnp.mean(x, axis=-1, keepdims=True)
    xc = x - mu
    var = jnp.mean(xc * xc, axis=-1, keepdims=True)
    return xc * lax.rsqrt(var + LN_EPS) * g + b


def _rms_norm(x, g):
    return x * lax.rsqrt(jnp.mean(x * x, axis=-1, keepdims=True) + RMS_EPS) * g


def _mod_kernel(c_ref, w_ref, b_ref, o_ref):
    c = c_ref[...]
    c_act = c * jax.nn.sigmoid(c)
    o_ref[...] = jnp.dot(c_act, w_ref[...], preferred_element_type=F32,
                         precision=lax.Precision.HIGHEST) + b_ref[...]


def _modulation(c, ada_w, ada_b):
    b, d = c.shape
    n = ada_w.shape[-1] // d
    return pl.pallas_call(
        _mod_kernel,
        grid=(DEPTH, n),
        in_specs=[pl.BlockSpec((b, d), lambda l, j: (0, 0)),
                  pl.BlockSpec((None, d, d), lambda l, j: (l, 0, j)),
                  pl.BlockSpec((None, 1, d), lambda l, j: (l, 0, j))],
        out_specs=pl.BlockSpec((None, b, d), lambda l, j: (l, 0, j)),
        out_shape=jax.ShapeDtypeStruct((DEPTH, b, n * d), F32),
        compiler_params=_cparams(("arbitrary", "arbitrary")),
        name="modulation",
    )(c, ada_w, ada_b.reshape(DEPTH, 1, n * d))


def _mix_prep_kernel(*refs, do_ln, has_vres, per_b):
    refs = list(refs)
    take = lambda n: [refs.pop(0) for _ in range(n)]
    x_ref, xprev_ref, lng_ref, lnb_ref, sc_ref, sh_ref, w_ref = take(7)
    mla_in = take(8)
    rwkv_in = take(13 if has_vres else 9)
    xn_ref = take(1)[0] if do_ln else None
    mla_out = take(3)
    rwkv_out = take(8)

    def modulated(x):
        if do_ln:
            x = _layer_norm(x, lng_ref[...], lnb_ref[...])
        return x, x * (1.0 + sc_ref[...]) + sh_ref[...]

    x, h = modulated(x_ref[...])
    if do_ln:
        xn_ref[...] = x
    _, h_prev = modulated(xprev_ref[...])
    p_all = jnp.dot(jnp.concatenate([h_prev, h], axis=0).astype(BF16), w_ref[...],
                    preferred_element_type=F32)
    p = p_all[PREV_ROWS:]
    first = pl.program_id(0) % per_b == 0
    prev_row = jnp.where(first, 0.0, p_all[PREV_ROWS - 1:PREV_ROWS, PM_COLS:])
    _mla_operands(p[:, :PM_COLS], *mla_in, *mla_out)
    _rwkv_operands(p[:, PM_COLS:], prev_row, rwkv_in, rwkv_out, has_vres)


def _mix_prep(x, ln_g, ln_b, sc, sh, w, mla_args, rwkv_args, vres, batch, seq, tm, tq, do_ln):
    t, d = x.shape
    per_b = seq // tm
    hw = MLA_HEADS * LANES
    c = RWKV_DIM
    row = lambda i: (i, 0)
    const = lambda i: (0, 0)
    mod = lambda i: (i // per_b, 0, 0)
    prev = lambda i: (jnp.maximum(i * (tm // PREV_ROWS) - 1, 0), 0)
    full = lambda a: pl.BlockSpec(a.shape, const)
    pos, q_g, kv_g, wq2, wk, wvt, invf, sgn = mla_args
    in_specs = [pl.BlockSpec((tm, d), row), pl.BlockSpec((PREV_ROWS, d), prev), full(ln_g), full(ln_b),
                pl.BlockSpec((None, 1, d), mod), pl.BlockSpec((None, 1, d), mod), full(w),
                pl.BlockSpec((tm, 1), row)] + [full(a) for a in mla_args[1:]]
    in_specs += [full(a) for a in rwkv_args]
    args = [x, x, ln_g, ln_b, sc, sh, w, *mla_args, *rwkv_args]
    if vres is not None:
        in_specs += [pl.BlockSpec((tm, c), row)] + [full(a) for a in vres[1:]]
        args += list(vres)
    sub = tq // tm
    out_specs = [pl.BlockSpec((tm, hw), row), pl.BlockSpec((tm, hw), row),
                 pl.BlockSpec((None, None, hw, tm),
                              lambda i: (i // per_b, (i % per_b) // sub, 0, (i % per_b) % sub))]
    out_shape = [jax.ShapeDtypeStruct((t, hw), BF16), jax.ShapeDtypeStruct((t, hw), BF16),
                 jax.ShapeDtypeStruct((batch, seq // tq, hw, tq), BF16)]
    out_specs += [pl.BlockSpec((tm, c), row)] * 8
    out_shape += [jax.ShapeDtypeStruct((t, c), F32)] * 8
    if do_ln:
        out_specs = [pl.BlockSpec((tm, d), row)] + out_specs
        out_shape = [jax.ShapeDtypeStruct((t, d), F32)] + out_shape
    return pl.pallas_call(
        functools.partial(_mix_prep_kernel, do_ln=do_ln, has_vres=vres is not None, per_b=per_b),
        grid=(t // tm,),
        in_specs=in_specs,
        out_specs=out_specs,
        out_shape=out_shape,
        compiler_params=_cparams(("parallel",)),
        name="mix_prep",
    )(*args)


def _mla_operands(pm, pos_ref, qg_ref, kvg_ref, wq_ref, wk_ref, wvt_ref, invf_ref, sgn_ref,
                  q_ref, k_ref, vt_ref):
    ang = pos_ref[...].astype(F32) * invf_ref[...]
    cos = jnp.cos(ang)
    sin = jnp.sin(ang) * sgn_ref[...]
    qn = _rms_norm(pm[:, :Q_LORA], qg_ref[...])
    q2 = jnp.dot(qn.astype(BF16), wq_ref[...], preferred_element_type=F32)
    kvn = _rms_norm(pm[:, Q_LORA:Q_LORA + KV_LORA], kvg_ref[...])
    kv = jnp.dot(kvn.astype(BF16), wk_ref[...], preferred_element_type=F32)
    v_t = jnp.dot(wvt_ref[...], kvn.T.astype(BF16), preferred_element_type=F32)
    ones_hi = (lax.broadcasted_iota(I32, (v_t.shape[0], 1), 0) % LANES >= V_HEAD).astype(F32)
    vt_ref[...] = (v_t + ones_hi).astype(BF16)
    off = Q_LORA + KV_LORA
    k_rot = pm[:, off:off + LANES] * cos + pm[:, off + LANES:off + 2 * LANES] * sin
    scale = QK_HEAD ** -0.5
    hw = MLA_HEADS * LANES
    for h in range(MLA_HEADS):
        sl = slice(h * LANES, (h + 1) * LANES)
        sl2 = slice(hw + h * LANES, hw + (h + 1) * LANES)
        q_ref[:, sl] = ((q2[:, sl] * cos + q2[:, sl2] * sin) * scale).astype(BF16)
        k_ref[:, sl] = (kv[:, sl] + k_rot).astype(BF16)


def _attn_kernel(q_ref, k_ref, vt_ref, o_ref, *, tq):
    qi = pl.program_id(2)
    key = lax.broadcasted_iota(I32, (tq, tq), 0)
    qry = lax.broadcasted_iota(I32, (tq, tq), 1)
    causal = key <= qry
    heads = range(ATTN_GROUP)
    qs = [q_ref[:, h * LANES:(h + 1) * LANES] for h in heads]

    def block(j, carry, masked):
        ms, accs = carry
        start = pl.multiple_of(j * tq, tq)
        ks = [k_ref[pl.ds(start, tq), h * LANES:(h + 1) * LANES] for h in heads]
        vts = [vt_ref[j, h * LANES:(h + 1) * LANES, :] for h in heads]
        ss = [lax.dot_general(ks[h], qs[h], (((1,), (1,)), ((), ())), preferred_element_type=F32)
              for h in heads]
        if masked:
            ss = [jnp.where(causal, s, -jnp.inf) for s in ss]
        m_new = [jnp.maximum(ms[h], jnp.max(ss[h], axis=0, keepdims=True)) for h in heads]
        alpha = [jnp.exp(ms[h] - m_new[h]) for h in heads]
        ps = [jnp.exp(ss[h] - m_new[h]).astype(BF16) for h in heads]
        acc_new = [alpha[h] * accs[h] + jnp.dot(vts[h], ps[h], preferred_element_type=F32) for h in heads]
        return tuple(m_new), tuple(acc_new)

    init = (tuple(jnp.full((1, tq), -jnp.inf, F32) for _ in heads),
            tuple(jnp.zeros((LANES, tq), F32) for _ in heads))
    carry = lax.fori_loop(0, qi, functools.partial(block, masked=False), init)
    _, accs = block(qi, carry, True)
    outs = [a[:V_HEAD] / a[V_HEAD:V_HEAD + 1] for a in accs]
    for p in range(ATTN_GROUP // 2):
        pair = jnp.concatenate([outs[2 * p], outs[2 * p + 1]], axis=0)
        o_ref[:, p * LANES:(p + 1) * LANES] = pair.T


def _attention(q, k, vt, batch, seq, tq):
    t = q.shape[0]
    nq = seq // tq
    gw = ATTN_GROUP * LANES
    return pl.pallas_call(
        functools.partial(_attn_kernel, tq=tq),
        grid=(batch, MLA_HEADS // ATTN_GROUP, nq),
        in_specs=[pl.BlockSpec((tq, gw), lambda b, h, i: (b * nq + i, h)),
                  pl.BlockSpec((seq, gw), lambda b, h, i: (b, h)),
                  pl.BlockSpec((None, nq, gw, tq), lambda b, h, i: (b, 0, h, 0))],
        out_specs=pl.BlockSpec((tq, gw // 2), lambda b, h, i: (b * nq + i, h)),
        out_shape=jax.ShapeDtypeStruct((t, MLA_OUT), F32),
        compiler_params=_cparams(("parallel", "parallel", "arbitrary")),
        name="attention",
    )(q, k, vt)


def _head_ones(n):
    r = lax.broadcasted_iota(I32, (n, n), 0) // RWKV_N
    c = lax.broadcasted_iota(I32, (n, n), 1) // RWKV_N
    return (r == c).astype(BF16)


def _head_sums(x):
    gw = RWKV_GROUP * RWKV_N
    ones = _head_ones(gw)
    return jnp.concatenate([_split_dot(x[:, g * gw:(g + 1) * gw], ones) for g in range(x.shape[1] // gw)],
                           axis=1)


def _rwkv_operands(p, prev_row, in_refs, out_refs, has_vres):
    if has_vres:
        (mu_ref, w0_ref, w2_ref, a0_ref, a2_ref, g2_ref, kk_ref, ka_ref, rk_ref,
         vf_ref, v0_ref, v1_ref, v2_ref) = in_refs
    else:
        mu_ref, w0_ref, w2_ref, a0_ref, a2_ref, g2_ref, kk_ref, ka_ref, rk_ref = in_refs
    r_out, lw_out, k_out, v_out, al_out, be_out, g_out, bonus_out = out_refs
    row = lax.broadcasted_iota(I32, p.shape, 0)
    p_prev = jnp.where(row == 0, prev_row, pltpu.roll(p, 1, 0))
    p = p + mu_ref[...] * (p_prev - p)
    c = RWKV_DIM
    r = p[:, :c]
    k = p[:, c:2 * c]
    v = p[:, 2 * c:3 * c]
    wd = p[:, 3 * c:3 * c + DECAY_LORA]
    ad = p[:, 3 * c + DECAY_LORA:3 * c + DECAY_LORA + ICLR_LORA]
    gd = p[:, 3 * c + DECAY_LORA + ICLR_LORA:]
    z = w0_ref[...] + _bdot(jnp.tanh(wd), w2_ref[...])
    y = -z
    softplus = jnp.maximum(y, 0.0) + jnp.log(1.0 + jnp.exp(-jnp.abs(y)))
    lw_out[...] = -jnp.exp(-softplus - 0.5)
    a = jax.nn.sigmoid(a0_ref[...] + _bdot(ad, a2_ref[...]))
    g_out[...] = _bdot(jax.nn.sigmoid(gd), g2_ref[...])
    if has_vres:
        mix = jax.nn.sigmoid(v0_ref[...] + _bdot(_bdot(v, v1_ref[...]), v2_ref[...]))
        v = v + (vf_ref[...] - v) * mix
    kk = k * kk_ref[...]
    norm = jnp.sqrt(_head_sums(kk * kk))
    kk = kk / jnp.maximum(norm, 1e-12)
    k = k * (1.0 + (a - 1.0) * ka_ref[...])
    r_out[...] = r
    k_out[...] = k
    v_out[...] = v
    al_out[...] = -kk
    be_out[...] = kk * a
    bonus_out[...] = _head_sums(r * k * rk_ref[...]) * v


def _rwkv_rec_kernel(r_ref, lw_ref, k_ref, v_ref, al_ref, be_ref, o_ref, st_ref, *, nb):
    cs = CHUNK
    gw = RWKV_GROUP * RWKV_N
    ng = RWKV_DIM // gw

    @pl.when(pl.program_id(1) == 0)
    def _():
        st_ref[...] = jnp.zeros_like(st_ref)

    row = lax.broadcasted_iota(I32, (cs, gw), 0)
    lane = lax.broadcasted_iota(I32, (cs, gw), 1)
    col = lane % cs
    head = lane // RWKV_N
    strict = col < row
    incl = col <= row
    eye_g = (col == row).astype(F32)
    r2 = lax.broadcasted_iota(I32, (gw, gw), 0)
    l2 = lax.broadcasted_iota(I32, (gw, gw), 1)
    same_head = (r2 // RWKV_N) == (l2 // RWKV_N)
    eye_sq = r2 == l2
    tri = (lax.broadcasted_iota(I32, (cs, cs), 1) <= lax.broadcasted_iota(I32, (cs, cs), 0)).astype(BF16)

    def bd(x):
        return jnp.concatenate([jnp.where(head == h, x, 0.0) for h in range(RWKV_GROUP)],
                               axis=0).astype(BF16)

    streams = [(b, g) for b in range(nb) for g in range(ng)]
    ld = lambda ref: [ref[b, :, g * gw:(g + 1) * gw] for b, g in streams]
    r, lw, k, v, al, be = ld(r_ref), ld(lw_ref), ld(k_ref), ld(v_ref), ld(al_ref), ld(be_ref)
    each = lambda f, *ls: [f(*xs) for xs in zip(*ls)]

    cum = each(lambda x: _split_dot_left(tri, x), lw)
    cum_end = each(lambda c_: c_[cs - 1:cs, :], cum)
    e_neg = each(lambda c_: jnp.exp(-c_), cum)
    e_end = each(lambda c_, ce: jnp.exp(ce - c_), cum, cum_end)
    a_bar = each(lambda a_, c_, w_: a_ * jnp.exp(c_ - w_), al, cum, lw)
    r_bar = each(lambda r_, c_: r_ * jnp.exp(c_), r, cum)
    b_bar = each(lambda x, e: x * e, be, e_neg)
    k_bar = each(lambda x, e: x * e, k, e_neg)
    b_til = each(lambda x, e: x * e, be, e_end)
    k_til = each(lambda x, e: x * e, k, e_end)
    gram = each(lambda a_, r_, b_, k_: _bdot_nt(jnp.concatenate([a_, r_], axis=0),
                                                jnp.concatenate([bd(b_), bd(k_)], axis=0)),
                a_bar, r_bar, b_bar, k_bar)
    l_ab = each(lambda g_: jnp.where(strict, g_[:cs, :gw], 0.0), gram)
    l_ak = each(lambda g_: jnp.where(strict, g_[:cs, gw:], 0.0), gram)
    m_rb = each(lambda g_: jnp.where(incl, g_[cs:, :gw], 0.0), gram)
    m_rk = each(lambda g_: jnp.where(incl, g_[cs:, gw:], 0.0), gram)
    x = each(lambda l_: eye_g + l_, l_ab)
    lp = each(lambda l_: _bdot(l_, bd(l_)), l_ab)
    for _ in range(4):
        xp = each(lambda x_, l_: _bdot(jnp.concatenate([x_, l_], axis=0), bd(l_)), x, lp)
        x = each(lambda x_, t_: x_ + t_[:cs], x, xp)
        lp = each(lambda t_: t_[cs:], xp)
    x = each(lambda x_, l_: x_ + _bdot(x_, bd(l_)), x, lp)
    lmv = each(lambda l_, m_, v_: _bdot(jnp.concatenate([l_, m_], axis=0), bd(v_)), l_ak, m_rk, v)
    lv = each(lambda t_: t_[:cs], lmv)
    mv = each(lambda t_: t_[cs:], lmv)
    w1 = each(lambda x_, a_: _bdot(x_, bd(a_)), x, a_bar)
    u0 = each(lambda x_, l_: _bdot(x_, bd(l_)), x, lv)
    bk_t = each(lambda b_, k_: jnp.concatenate([b_, k_], axis=0).T.astype(BF16), b_til, k_til)
    pc_col = each(lambda ce: jnp.sum(jnp.where(eye_sq, jnp.exp(ce), 0.0), axis=1, keepdims=True), cum_end)
    sb = [st_ref[s] for s in range(len(streams))]
    ws = each(lambda w_, r_, s_: _bdot(jnp.concatenate([w_, r_], axis=0), s_), w1, r_bar, sb)
    u = each(lambda w_, u_: w_[:cs] + u_, ws, u0)
    out = each(lambda w_, m_, u_, mv_: w_[cs:] + _bdot(m_, bd(u_)) + mv_, ws, m_rb, u, mv)
    upd = each(lambda t_, u_, v_: _bdot(t_, jnp.concatenate([u_, v_], axis=0)), bk_t, u, v)
    for s, (b, g) in enumerate(streams):
        o_ref[b, :, g * gw:(g + 1) * gw] = out[s]
        st_ref[s] = pc_col[s] * sb[s] + jnp.where(same_head, upd[s], 0.0)


def _split_dot_left(ones_bf16, x):
    hi = x.astype(BF16)
    lo = (x - hi.astype(F32)).astype(BF16)
    return (jnp.dot(ones_bf16, hi, preferred_element_type=F32)
            + jnp.dot(ones_bf16, lo, preferred_element_type=F32))


def _rwkv_recurrence(r, lw, k, v, al, be, batch, seq, nb):
    t, c = r.shape
    nc = seq // CHUNK
    gw = RWKV_GROUP * RWKV_N
    blk = pl.BlockSpec((nb, CHUNK, c), lambda b, j: (b, j, 0))
    view = lambda a: a.reshape(batch, seq, c)
    out = pl.pallas_call(
        functools.partial(_rwkv_rec_kernel, nb=nb),
        grid=(batch // nb, nc),
        in_specs=[blk] * 6,
        out_specs=blk,
        out_shape=jax.ShapeDtypeStruct((batch, seq, c), F32),
        scratch_shapes=[pltpu.VMEM((nb * (c // gw), gw, gw), F32)],
        compiler_params=_cparams(("parallel", "arbitrary")),
        name="rwkv_recurrence",
    )(view(r), view(lw), view(k), view(v), view(al), view(be))
    return out.reshape(t, c)


def _outproj_kernel(x_ref, att_ref, rec_ref, bonus_ref, g_ref, og_ref, gng_ref, gnb_ref, wo_ref,
                    gt_ref, lng_ref, lnb_ref, sc_ref, sh_ref, rwh_ref, rwl_ref, rb_ref,
                    x1_ref, h2_ref, pg_ref, pgt_ref, meta_ref, run_ref):
    step = pl.program_id(0)

    @pl.when(step == 0)
    def _():
        run_ref[...] = jnp.zeros_like(run_ref)

    tm = x_ref.shape[0]
    mla = _rms_norm(att_ref[...], og_ref[...])
    o = rec_ref[...]
    mean = _head_sums(o) * (1.0 / RWKV_N)
    oc = o - mean
    var = _head_sums(oc * oc) * (1.0 / RWKV_N)
    rw = (oc * lax.rsqrt(var + RWKV_GN_EPS) * gng_ref[...] + gnb_ref[...] + bonus_ref[...]) * g_ref[...]
    mix = (jnp.dot(mla.astype(BF16), wo_ref[:MLA_OUT, :], preferred_element_type=F32)
           + jnp.dot(rw.astype(BF16), wo_ref[MLA_OUT:, :], preferred_element_type=F32))
    x1 = _layer_norm(DEEPNORM_ALPHA * x_ref[...] + (1.0 + gt_ref[...]) * mix, lng_ref[...], lnb_ref[...])
    x1_ref[...] = x1
    h2 = x1 * (1.0 + sc_ref[...]) + sh_ref[...]
    h2_ref[...] = h2.astype(BF16)
    h_hi = h2.astype(BF16)
    h_lo = (h2 - h_hi.astype(F32)).astype(BF16)
    logits = (_bdot_nt(rwh_ref[...], h_hi) + _bdot_nt(rwh_ref[...], h_lo) + _bdot_nt(rwl_ref[...], h_hi)
              + rb_ref[...])
    ne = N_EXPERTS
    e_iota = lax.broadcasted_iota(I32, (ne, tm), 0)
    work = logits
    vals, hots = [], []
    for kk in range(TOP_K):
        m = jnp.max(work, axis=0, keepdims=True)
        sel = jnp.min(jnp.where(work == m, e_iota, ne), axis=0, keepdims=True)
        hot = e_iota == sel
        vals.append(m)
        hots.append(hot)
        work = jnp.where(hot, -jnp.inf, work)
    exps = [jnp.exp(vv - vals[0]) for vv in vals]
    denom = exps[0] + exps[1] + exps[2] + exps[3]
    any_hot = (hots[0] | hots[1] | hots[2] | hots[3]).astype(BF16)
    earlier = (lax.broadcasted_iota(I32, (tm, tm), 0) < lax.broadcasted_iota(I32, (tm, tm), 1)).astype(BF16)
    before = jnp.dot(any_hot, earlier, preferred_element_type=F32)
    cnt_col = jnp.sum(any_hot.astype(F32), axis=1, keepdims=True)
    lower = (lax.broadcasted_iota(I32, (ne, ne), 1) < lax.broadcasted_iota(I32, (ne, ne), 0)).astype(BF16)
    excl_col = _split_dot_left(lower, jnp.broadcast_to(cnt_col, (ne, LANES)))[:, 0:1]
    base = before + excl_col
    rows = [jnp.sum(jnp.where(hots[kk], base, 0.0), axis=0, keepdims=True) for kk in range(TOP_K)]
    rows += [exps[kk] / denom for kk in range(TOP_K)]
    pg_t = jnp.concatenate(rows, axis=0)
    pgt_ref[...] = pg_t
    pad = jnp.zeros((LANES - 2 * TOP_K, tm), F32)
    pg_ref[...] = jnp.concatenate([pg_t, pad], axis=0).T
    eye = lax.broadcasted_iota(I32, (ne, ne), 0) == lax.broadcasted_iota(I32, (ne, ne), 1)
    as_row = lambda col: jnp.sum(jnp.where(eye, col, 0.0), axis=0, keepdims=True)
    tile_cnt = as_row(cnt_col)
    mrow = lax.broadcasted_iota(I32, (SUBLANES, ne), 0)
    meta = jnp.where(mrow == 0, tile_cnt,
                     jnp.where(mrow == 1, as_row(excl_col), jnp.where(mrow == 2, run_ref[...], 0.0)))
    meta_ref[...] = meta.astype(I32)
    run_ref[...] = run_ref[...] + tile_cnt


def _outproj(x, att, rec, bonus, g, out_g, gn_g, gn_b, wo, gt, ln_g, ln_b, sc, sh, rw, rb, seq, tm):
    t, d = x.shape
    rw_t = rw.T
    rw_hi = rw_t.astype(BF16)
    rw_lo = (rw_t - rw_hi.astype(F32)).astype(BF16)
    rb = rb.reshape(N_EXPERTS, 1)
    per_b = seq // tm
    row = lambda i: (i, 0)
    const = lambda i: (0, 0)
    mod = lambda i: (i // per_b, 0, 0)
    vec = lambda n: pl.BlockSpec((1, n), const)
    half = lambda: pl.BlockSpec((tm, RWKV_DIM), row)
    n_tiles = t // tm
    return pl.pallas_call(
        _outproj_kernel,
        grid=(t // tm,),
        in_specs=[pl.BlockSpec((tm, d), row), half(), half(), half(), half(),
                  vec(MLA_OUT), vec(RWKV_DIM), vec(RWKV_DIM), pl.BlockSpec(wo.shape, const),
                  pl.BlockSpec((None, 1, d), mod), vec(d), vec(d),
                  pl.BlockSpec((None, 1, d), mod), pl.BlockSpec((None, 1, d), mod),
                  pl.BlockSpec(rw_hi.shape, const), pl.BlockSpec(rw_lo.shape, const),
                  pl.BlockSpec((N_EXPERTS, 1), const)],
        out_specs=[pl.BlockSpec((tm, d), row), pl.BlockSpec((tm, d), row),
                   pl.BlockSpec((tm, LANES), row),
                   pl.BlockSpec((None, 2 * TOP_K, tm), lambda i: (i, 0, 0)),
                   pl.BlockSpec((None, SUBLANES, N_EXPERTS), lambda i: (i, 0, 0))],
        out_shape=[jax.ShapeDtypeStruct((t, d), F32), jax.ShapeDtypeStruct((t, d), BF16),
                   jax.ShapeDtypeStruct((t, LANES), F32),
                   jax.ShapeDtypeStruct((n_tiles, 2 * TOP_K, tm), F32),
                   jax.ShapeDtypeStruct((n_tiles, SUBLANES, N_EXPERTS), I32)],
        scratch_shapes=[pltpu.VMEM((1, N_EXPERTS), F32)],
        compiler_params=_cparams(("arbitrary",)),
        name="outproj_router",
    )(x, att, rec, bonus, g, out_g, gn_g, gn_b, wo, gt, ln_g, ln_b, sc, sh, rw_hi, rw_lo, rb)


def _segment_copies(count, src_row, dst_row, src_ref, dst_ref, sem, max_bits, act):
    def bits(lo, hi):
        for b in range(lo, hi):
            size = 1 << b

            @pl.when((count & size) != 0)
            def _():
                off = count & (size - 1)
                s = pl.multiple_of((src_row + off) * SUBLANES, SUBLANES)
                d = pl.multiple_of((dst_row + off) * SUBLANES, SUBLANES)
                act(pltpu.make_async_copy(src_ref.at[pl.ds(s, size * SUBLANES)],
                                          dst_ref.at[pl.ds(d, size * SUBLANES)], sem))

    split = min(max_bits, SEG_COMMON_BITS)
    bits(0, split)
    if max_bits > split:
        @pl.when(count >= (1 << split))
        def _():
            bits(split, max_bits)


def _start(copy):
    copy.start()


def _wait(copy):
    copy.wait()


def _dispatch_kernel(cnt_ref, src_ref, dst_ref, zcnt_ref, zdst_ref, nused_ref, h_ref, pos_ref, xs_hbm,
                     sorted_ref, zeros_ref, sems, zsem, *, ts, bm):
    tau = pl.program_id(0)
    n_tiles = pl.num_programs(0)
    slot = tau % 2
    n_rows = TOP_K * ts
    seg_bits = n_rows.bit_length()
    pad_bits = (bm - 1).bit_length()

    def drain(sl):
        pltpu.make_async_copy(sorted_ref.at[sl], xs_hbm.at[pl.ds(0, n_rows * SUBLANES)], sems.at[sl]).wait()

    @pl.when(tau >= 2)
    def _():
        drain(slot)

    pos_t = pos_ref[...]
    a = lax.broadcasted_iota(I32, (n_rows, ts), 0).astype(F32)
    perm = (pos_t[0:1] == a) | (pos_t[1:2] == a) | (pos_t[2:3] == a) | (pos_t[3:4] == a)
    srt = jnp.dot(perm.astype(BF16), h_ref[...], preferred_element_type=F32)
    for j in range(D_MODEL // LANES):
        sorted_ref[slot, pl.ds(j, n_rows, stride=SUBLANES), :] = srt[:, j * LANES:(j + 1) * LANES]

    def issue(e, carry):
        i = tau * N_EXPERTS + e
        _segment_copies(cnt_ref[i], src_ref[i], dst_ref[i], sorted_ref.at[slot], xs_hbm, sems.at[slot],
                        seg_bits, _start)
        return carry

    lax.fori_loop(0, N_EXPERTS, issue, 0)

    @pl.when(tau == 0)
    def _():
        zeros_ref[...] = jnp.zeros_like(zeros_ref)

        def pad(act):
            def body(e, carry):
                _segment_copies(zcnt_ref[e], 0, zdst_ref[e], zeros_ref, xs_hbm, zsem, pad_bits, act)
                return carry
            lax.fori_loop(0, N_EXPERTS, body, 0)

        def tail(act):
            def body(blk, carry):
                row0 = pl.multiple_of(blk * (bm * SUBLANES), bm * SUBLANES)
                act(pltpu.make_async_copy(zeros_ref, xs_hbm.at[pl.ds(row0, bm * SUBLANES)], zsem))
                return carry
            lax.fori_loop(nused_ref[0], xs_hbm.shape[0] // (bm * SUBLANES), body, 0)

        pad(_start)
        tail(_start)
        pad(_wait)
        tail(_wait)

    @pl.when(tau == n_tiles - 1)
    def _():
        drain(slot)

    @pl.when((tau == n_tiles - 1) & (tau >= 1))
    def _():
        drain(1 - slot)


def _dispatch(seg_cnt, seg_src, seg_dst, pad_cnt, pad_dst, n_used, h2, pos, n_slots, ts, bm):
    t, d = h2.shape
    n_rows = TOP_K * ts
    grid_spec = pltpu.PrefetchScalarGridSpec(
        num_scalar_prefetch=6,
        grid=(t // ts,),
        in_specs=[pl.BlockSpec((ts, d), lambda i, *_: (i, 0)),
                  pl.BlockSpec((None, 2 * TOP_K, ts), lambda i, *_: (i, 0, 0))],
        out_specs=pl.BlockSpec(memory_space=pl.ANY),
        scratch_shapes=[pltpu.VMEM((2, n_rows * SUBLANES, LANES), F32),
                        pltpu.VMEM((bm * SUBLANES, LANES), F32),
                        pltpu.SemaphoreType.DMA((2,)), pltpu.SemaphoreType.DMA(())],
    )
    return pl.pallas_call(
        functools.partial(_dispatch_kernel, ts=ts, bm=bm),
        grid_spec=grid_spec,
        out_shape=jax.ShapeDtypeStruct((n_slots * SUBLANES, LANES), F32),
        compiler_params=_cparams(("arbitrary",)),
        name="dispatch",
    )(seg_cnt, seg_src, seg_dst, pad_cnt, pad_dst, n_used, h2, pos)


def _ffn_kernel(be_ref, nused_ref, x_ref, w1_ref, b1_ref, w2_ref, b2_ref, y_ref, w1s_ref, w2s_ref, *, bm):
    i = pl.program_id(0)
    f = w2_ref.shape[0]
    used = i < nused_ref[0]
    new_expert = (i == 0) | (be_ref[i] != be_ref[jnp.maximum(i - 1, 0)])

    @pl.when(used & new_expert)
    def _():
        grp = 2 * LANES
        c = lax.broadcasted_iota(I32, (grp, grp), 0)
        j = lax.broadcasted_iota(I32, (grp, grp), 1)
        perm = (c == jnp.where(j < LANES, 2 * j, 2 * (j - LANES) + 1)).astype(BF16)
        for g in range(w1_ref.shape[1] // grp):
            y = jnp.dot(w1_ref[:, g * grp:(g + 1) * grp].astype(BF16), perm, preferred_element_type=F32)
            w1s_ref[:, g * LANES:(g + 1) * LANES] = y[:, :LANES].astype(BF16)
            w1s_ref[:, f + g * LANES:f + (g + 1) * LANES] = y[:, LANES:].astype(BF16)
        w2s_ref[...] = w2_ref[...].astype(BF16)

    @pl.when(used)
    def _():
        x = jnp.concatenate([x_ref[pl.ds(j, bm, stride=SUBLANES), :] for j in range(D_MODEL // LANES)],
                            axis=1)
        hh = jnp.dot(x.astype(BF16), w1s_ref[...], preferred_element_type=F32) + b1_ref[...]
        x_glu = jnp.minimum(hh[:, :f], SWIGLU_LIMIT)
        x_lin = jnp.clip(hh[:, f:], -SWIGLU_LIMIT, SWIGLU_LIMIT)
        u = x_glu * jax.nn.sigmoid(SWIGLU_ALPHA * x_glu) * (x_lin + 1.0)
        y = jnp.dot(u.astype(BF16), w2s_ref[...], preferred_element_type=F32) + b2_ref[...]
        for j in range(D_MODEL // LANES):
            y_ref[pl.ds(j, bm, stride=SUBLANES), :] = y[:, j * LANES:(j + 1) * LANES]

    @pl.when(i >= nused_ref[0])
    def _():
        y_ref[...] = jnp.zeros_like(y_ref)


def _expert_ffn(block_expert, n_used, xs, w1, b1, w2, b2, layer, bm):
    n_blocks = block_expert.shape[0]
    d = w1.shape[2]
    f = w2.shape[2]
    last_used = lambda i, nu: jnp.maximum(jnp.minimum(i, nu[0] - 1), 0)
    of_expert = lambda i, be, nu: (layer, be[i], 0, 0)
    grid_spec = pltpu.PrefetchScalarGridSpec(
        num_scalar_prefetch=2,
        grid=(n_blocks,),
        in_specs=[pl.BlockSpec((bm * SUBLANES, LANES), lambda i, be, nu: (last_used(i, nu), 0)),
                  pl.BlockSpec((None, None, d, 2 * f), of_expert),
                  pl.BlockSpec((None, None, 1, 2 * f), of_expert),
                  pl.BlockSpec((None, None, f, d), of_expert),
                  pl.BlockSpec((None, None, 1, d), of_expert)],
        out_specs=pl.BlockSpec((bm * SUBLANES, LANES), lambda i, be, nu: (i, 0)),
        scratch_shapes=[pltpu.VMEM((d, 2 * f), BF16), pltpu.VMEM((f, d), BF16)],
    )
    return pl.pallas_call(
        functools.partial(_ffn_kernel, bm=bm),
        grid_spec=grid_spec,
        out_shape=jax.ShapeDtypeStruct(xs.shape, F32),
        compiler_params=_cparams(("arbitrary",)),
        name="expert_ffn",
    )(block_expert, n_used, xs, w1, b1, w2, b2)


def _combine_kernel(cnt_ref, src_ref, dst_ref, y_hbm, x_ref, pg_ref, gt_ref, lng_ref, lnb_ref,
                    o_ref, ybuf, sems, *, ts):
    tau = pl.program_id(0)
    n_tiles = pl.num_programs(0)
    slot = tau % 2
    n_rows = TOP_K * ts
    seg_bits = n_rows.bit_length()

    def fetch(tile, sl):
        def body(e, carry):
            i = tile * N_EXPERTS + e
            _segment_copies(cnt_ref[i], dst_ref[i], src_ref[i], y_hbm, ybuf.at[sl], sems.at[sl],
                            seg_bits, _start)
            return carry
        lax.fori_loop(0, N_EXPERTS, body, 0)

    @pl.when(tau == 0)
    def _():
        fetch(0, 0)

    @pl.when(tau + 1 < n_tiles)
    def _():
        fetch(tau + 1, 1 - slot)

    pltpu.make_async_copy(y_hbm.at[pl.ds(0, n_rows * SUBLANES)], ybuf.at[slot], sems.at[slot]).wait()
    y = jnp.concatenate([ybuf[slot, pl.ds(j, n_rows, stride=SUBLANES), :] for j in range(D_MODEL // LANES)],
                        axis=1)
    pg = pg_ref[...]
    a = lax.broadcasted_iota(I32, (ts, n_rows), 1).astype(F32)
    gmat = jnp.zeros((ts, n_rows), F32)
    for kk in range(TOP_K):
        gmat = gmat + jnp.where(pg[:, kk:kk + 1] == a, pg[:, TOP_K + kk:TOP_K + kk + 1], 0.0)
    g_hi = gmat.astype(BF16)
    g_lo = (gmat - g_hi.astype(F32)).astype(BF16)
    y_hi = y.astype(BF16)
    y_lo = (y - y_hi.astype(F32)).astype(BF16)
    ffn = (jnp.dot(g_hi, y_hi, preferred_element_type=F32) + jnp.dot(g_hi, y_lo, preferred_element_type=F32)
           + jnp.dot(g_lo, y_hi, preferred_element_type=F32))
    o_ref[...] = _layer_norm(DEEPNORM_ALPHA * x_ref[...] + (1.0 + gt_ref[...]) * ffn,
                             lng_ref[...], lnb_ref[...])


def _combine(seg_cnt, seg_src, seg_dst, y, x1, pg, gt, ln_g, ln_b, seq, ts):
    t, d = x1.shape
    per_b = seq // ts
    n_rows = TOP_K * ts
    row = lambda i, *_: (i, 0)
    const = lambda i, *_: (0, 0)
    grid_spec = pltpu.PrefetchScalarGridSpec(
        num_scalar_prefetch=3,
        grid=(t // ts,),
        in_specs=[pl.BlockSpec(memory_space=pl.ANY),
                  pl.BlockSpec((ts, d), row), pl.BlockSpec((ts, LANES), row),
                  pl.BlockSpec((None, 1, d), lambda i, *_: (i // per_b, 0, 0)),
                  pl.BlockSpec((1, d), const), pl.BlockSpec((1, d), const)],
        out_specs=pl.BlockSpec((ts, d), row),
        scratch_shapes=[pltpu.VMEM((2, n_rows * SUBLANES, LANES), F32), pltpu.SemaphoreType.DMA((2,))],
    )
    return pl.pallas_call(
        functools.partial(_combine_kernel, ts=ts),
        grid_spec=grid_spec,
        out_shape=jax.ShapeDtypeStruct((t, d), F32),
        compiler_params=_cparams(("arbitrary",)),
        name="combine",
    )(seg_cnt, seg_src, seg_dst, y, x1, pg, gt, ln_g, ln_b)


def _rope_cols():
    half = QK_ROPE // 2
    return np.arange(half) * 2, np.arange(half) * 2 + 1


def _build_w_in(w_in):
    d = w_in.shape[0]
    even, odd = _rope_cols()
    kr = w_in[:, Q_LORA + KV_LORA:MLA_IN]
    zeros = lambda n: jnp.zeros((d, n), w_in.dtype)
    placed = jnp.concatenate([zeros(QK_NOPE), kr[:, even], kr[:, odd], zeros(LANES - QK_HEAD)], axis=1)
    swapped = jnp.concatenate([zeros(QK_NOPE), kr[:, odd], kr[:, even], zeros(LANES - QK_HEAD)], axis=1)
    return jnp.concatenate([w_in[:, :Q_LORA + KV_LORA], placed, swapped, w_in[:, MLA_IN:]],
                           axis=1).astype(BF16)


def _build_w_uq(w_uq):
    even, odd = _rope_cols()
    w = w_uq.reshape(Q_LORA, MLA_HEADS, QK_HEAD)
    nope = w[:, :, :QK_NOPE]
    rot = w[:, :, QK_NOPE:]
    pad = jnp.zeros((Q_LORA, MLA_HEADS, LANES - QK_HEAD), w_uq.dtype)
    zero_nope = jnp.zeros_like(nope)
    placed = jnp.concatenate([nope, rot[:, :, even], rot[:, :, odd], pad], axis=2)
    swapped = jnp.concatenate([zero_nope, rot[:, :, odd], rot[:, :, even], pad], axis=2)
    return jnp.concatenate([placed.reshape(Q_LORA, -1), swapped.reshape(Q_LORA, -1)], axis=1).astype(BF16)


def _build_w_ukv(w_uk, w_uv):
    w = w_uk.reshape(KV_LORA, MLA_HEADS, QK_NOPE)
    pad = jnp.zeros((KV_LORA, MLA_HEADS, LANES - QK_NOPE), w_uk.dtype)
    placed = jnp.concatenate([w, pad], axis=2).reshape(KV_LORA, -1)
    wv = w_uv.reshape(KV_LORA, MLA_HEADS, V_HEAD)
    vpad = jnp.zeros((KV_LORA, MLA_HEADS, LANES - V_HEAD), w_uv.dtype)
    v_placed = jnp.concatenate([wv, vpad], axis=2).reshape(KV_LORA, -1)
    return placed.astype(BF16), v_placed.T.astype(BF16)


def _rope_rows():
    half = QK_ROPE // 2
    inv = ROPE_THETA ** (-np.arange(0, QK_ROPE, 2, dtype=np.float32) / QK_ROPE)
    invf = np.zeros((1, LANES), np.float32)
    sgn = np.zeros((1, LANES), np.float32)
    invf[0, QK_NOPE:QK_NOPE + half] = inv
    invf[0, QK_NOPE + half:QK_HEAD] = inv
    sgn[0, QK_NOPE:QK_NOPE + half] = -1.0
    sgn[0, QK_NOPE + half:QK_HEAD] = 1.0
    return jnp.asarray(invf), jnp.asarray(sgn)


def kernel(x, c, positions, emb_ln_g, emb_ln_b, ada_w, ada_b, w_in, q_norm_g, w_uq, kv_norm_g, w_uk, w_uv, mla_out_g, rwkv_mu, rwkv_w0, rwkv_w2, rwkv_a0, rwkv_a2, rwkv_g2, rwkv_k_k, rwkv_k_a, rwkv_r_k, rwkv_gn_g, rwkv_gn_b, vres_v0, vres_v1, vres_v2, w_o, ln1_g, ln1_b, router_w, router_b, exp_w1, exp_b1, exp_w2, exp_b2, ln2_g, ln2_b):
    batch, seq, d = x.shape
    t = batch * seq
    tm = min(512, seq)
    tq = min(512, seq)
    ts = min(256, seq)
    bm = 512
    n_blocks = (t * TOP_K) // bm + N_EXPERTS
    n_slots = n_blocks * bm

    row = lambda a: a.reshape(1, -1)
    mod = _modulation(c, ada_w, ada_b)
    invf, sgn = _rope_rows()
    pos = positions.reshape(t, 1)
    xf = x.reshape(t, d)
    b1_all = jnp.concatenate([exp_b1[..., 0::2], exp_b1[..., 1::2]], axis=-1)[:, :, None, :]
    b2_all = exp_b2[:, :, None, :]
    v_first = None
    for i in range(DEPTH):
        sh1, sc1, gt1, sh2, sc2, gt2 = [m.reshape(batch, 1, d) for m in jnp.split(mod[i], 6, axis=-1)]
        wk, wvt = _build_w_ukv(w_uk[i], w_uv[i])
        mla_args = (pos, row(q_norm_g[i]), row(kv_norm_g[i]), _build_w_uq(w_uq[i]), wk, wvt, invf, sgn)
        rwkv_args = (row(rwkv_mu[i]), row(rwkv_w0[i]), rwkv_w2[i].astype(BF16), row(rwkv_a0[i]),
                     rwkv_a2[i].astype(BF16), rwkv_g2[i].astype(BF16), row(rwkv_k_k[i]),
                     row(rwkv_k_a[i]), row(rwkv_r_k[i]))
        vres = None if i == 0 else (v_first, row(vres_v0[i - 1]), vres_v1[i - 1].astype(BF16),
                                    vres_v2[i - 1].astype(BF16))
        outs = _mix_prep(xf, row(emb_ln_g), row(emb_ln_b), sc1, sh1, _build_w_in(w_in[i]), mla_args,
                         rwkv_args, vres, batch, seq, tm, tq, do_ln=(i == 0))
        if i == 0:
            xn, outs = outs[0], outs[1:]
        else:
            xn = xf
        q, k, vt, r_, lw, k_, v_, al, be, g_, bonus = outs
        att = _attention(q, k, vt, batch, seq, tq)
        if i == 0:
            v_first = v_
        rec = _rwkv_recurrence(r_, lw, k_, v_, al, be, batch, seq, nb=min(4, batch))
        x1, h2, pg, pg_t, meta = _outproj(
            xn, att, rec, bonus, g_, row(mla_out_g[i]), row(rwkv_gn_g[i]), row(rwkv_gn_b[i]),
            w_o[i].astype(BF16), gt1, row(ln1_g[i]), row(ln1_b[i]), sc2, sh2, router_w[i],
            row(router_b[i]), seq, ts)
        tile_cnt, tile_src, tile_before = meta[:, 0, :], meta[:, 1, :], meta[:, 2, :]
        cnt = tile_before[-1] + tile_cnt[-1]
        padded = (cnt + bm - 1) // bm * bm
        pad_ends = jnp.cumsum(padded)
        pad_starts = pad_ends - padded
        seg_cnt = tile_cnt.reshape(-1)
        seg_src = tile_src.reshape(-1)
        seg_dst = (pad_starts[None, :] + tile_before).reshape(-1)
        block_start = jnp.arange(n_blocks, dtype=I32) * bm
        block_expert = jnp.minimum(jnp.sum((pad_ends[None, :] <= block_start[:, None]).astype(I32), axis=1),
                                   N_EXPERTS - 1)
        n_used = (pad_ends[-1:] // bm).astype(I32)
        xs = _dispatch(seg_cnt, seg_src, seg_dst, padded - cnt, pad_starts + cnt, n_used, h2, pg_t,
                       n_slots, ts, bm)
        y = _expert_ffn(block_expert, n_used, xs, exp_w1, b1_all, exp_w2, b2_all, i, bm)
        xf = _combine(seg_cnt, seg_src, seg_dst, y, x1, pg, gt2, row(ln2_g[i]), row(ln2_b[i]), seq, ts)
    return xf.reshape(batch, seq, d)
```

```python
import functools

import jax
import jax.numpy as jnp
import numpy as np
from jax import lax
from jax.experimental import pallas as pl
from jax.experimental.pallas import tpu as pltpu

F32 = jnp.float32
BF16 = jnp.bfloat16
I32 = jnp.int32

D_MODEL = 1024
DEPTH = 2
LANES = 128
SUBLANES = 8

MLA_HEADS = 8
QK_NOPE = 64
QK_ROPE = 32
QK_HEAD = QK_NOPE + QK_ROPE
V_HEAD = 64
Q_LORA = 256
KV_LORA = 128
ROPE_THETA = 10000.0
LOG2_E = 1.4426950408889634
MLA_OUT = MLA_HEADS * V_HEAD
ATTN_GROUP = 4

RWKV_HEADS = 8
RWKV_N = 64
RWKV_PACK = ("r", "lw", "k", "v", "al", "be")
RWKV_GROUP = 4
RWKV_DIM = RWKV_HEADS * RWKV_N
DECAY_LORA = 64
ICLR_LORA = 64
GATE_LORA = 128
RWKV_GN_EPS = 64e-5
RWKV_IN = 3 * RWKV_DIM + DECAY_LORA + ICLR_LORA + GATE_LORA
MLA_IN = Q_LORA + KV_LORA + QK_ROPE

N_EXPERTS = 32
TOP_K = 4
SWIGLU_LIMIT = 7.0
SWIGLU_ALPHA = 1.702

DEEPNORM_ALPHA = (2 * DEPTH) ** 0.25
LN_EPS = 1e-5
RMS_EPS = 1e-6

PM_COLS = Q_LORA + KV_LORA + 2 * LANES
CHUNK = 64
PREV_ROWS = 16
SEG_COMMON_BITS = 6
VMEM_LIMIT = 56 * 1024 * 1024


def _tile_sizes(seq):
    tm = min(512, seq)
    tq = min(512, seq)
    ts = min(256, seq)
    bm = 512
    assert seq % tq == 0 and tq % tm == 0 and seq % ts == 0 and seq % CHUNK == 0
    return tm, tq, ts, bm


def _cparams(sem):
    return pltpu.CompilerParams(dimension_semantics=sem, vmem_limit_bytes=VMEM_LIMIT)


def _bdot(a, b):
    return jnp.dot(a.astype(BF16), b.astype(BF16), preferred_element_type=F32)


def _bdot_nt(a, b):
    return lax.dot_general(a.astype(BF16), b.astype(BF16), (((1,), (1,)), ((), ())),
                           preferred_element_type=F32)


def _split_dot(x, ones_bf16):
    hi = x.astype(BF16)
    lo = (x - hi.astype(F32)).astype(BF16)
    return (jnp.dot(hi, ones_bf16, preferred_element_type=F32)
            + jnp.dot(lo, ones_bf16, preferred_element_type=F32))


def _layer_norm(x, g, b):
    mu = jnp.mean(x, axis=-1, keepdims=True)
    xc = x - mu
    var = jnp.mean(xc * xc, axis=-1, keepdims=True)
    return xc * lax.rsqrt(var + LN_EPS) * g + b


def _rms_norm(x, g):
    return x * lax.rsqrt(jnp.mean(x * x, axis=-1, keepdims=True) + RMS_EPS) * g


def _mod_kernel(c_ref, w_ref, b_ref, o_ref):
    c = c_ref[...]
    c_act = c * jax.nn.sigmoid(c)
    o_ref[...] = jnp.dot(c_act, w_ref[...], preferred_element_type=F32,
                         precision=lax.Precision.HIGHEST) + b_ref[...]


def _modulation(c, ada_w, ada_b):
    b, d = c.shape
    n = ada_w.shape[-1] // d
    return pl.pallas_call(
        _mod_kernel,
        grid=(DEPTH, n),
        in_specs=[pl.BlockSpec((b, d), lambda l, j: (0, 0)),
                  pl.BlockSpec((None, d, d), lambda l, j: (l, 0, j)),
                  pl.BlockSpec((None, 1, d), lambda l, j: (l, 0, j))],
        out_specs=pl.BlockSpec((None, b, d), lambda l, j: (l, 0, j)),
        out_shape=jax.ShapeDtypeStruct((DEPTH, b, n * d), F32),
        compiler_params=_cparams(("arbitrary", "arbitrary")),
        name="modulation",
    )(c, ada_w, ada_b.reshape(DEPTH, 1, n * d))


def _mix_prep_kernel(*refs, do_ln, has_vres, per_b):
    refs = list(refs)
    take = lambda n: [refs.pop(0) for _ in range(n)]
    x_ref, xprev_ref, lng_ref, lnb_ref, sc_ref, sh_ref, w_ref = take(7)
    mla_in = take(8)
    rwkv_in = take(13 if has_vres else 9)
    xn_ref = take(1)[0] if do_ln else None
    mla_out = take(3)
    rwkv_out = take(3)

    def modulated(x):
        if do_ln:
            x = _layer_norm(x, lng_ref[...], lnb_ref[...])
        return x, x * (1.0 + sc_ref[...]) + sh_ref[...]

    x, h = modulated(x_ref[...])
    if do_ln:
        xn_ref[...] = x
    _, h_prev = modulated(xprev_ref[...])
    p_all = jnp.dot(jnp.concatenate([h_prev, h], axis=0).astype(BF16), w_ref[...],
                    preferred_element_type=F32)
    p = p_all[PREV_ROWS:]
    first = pl.program_id(0) % per_b == 0
    prev_row = jnp.where(first, 0.0, p_all[PREV_ROWS - 1:PREV_ROWS, PM_COLS:])
    _mla_operands(p[:, :PM_COLS], *mla_in, *mla_out)
    _rwkv_operands(p[:, PM_COLS:], prev_row, rwkv_in, rwkv_out, has_vres)


def _mix_prep(x, ln_g, ln_b, sc, sh, w, mla_args, rwkv_args, vres, batch, seq, tm, tq, do_ln):
    t, d = x.shape
    per_b = seq // tm
    hw = MLA_HEADS * LANES
    c = RWKV_DIM
    row = lambda i: (i, 0)
    const = lambda i: (0, 0)
    mod = lambda i: (i // per_b, 0, 0)
    prev = lambda i: (jnp.maximum(i * (tm // PREV_ROWS) - 1, 0), 0)
    full = lambda a: pl.BlockSpec(a.shape, const)
    pos, q_g, kv_g, wq2, wk, wvt, invf, sgn = mla_args
    in_specs = [pl.BlockSpec((tm, d), row), pl.BlockSpec((PREV_ROWS, d), prev), full(ln_g), full(ln_b),
                pl.BlockSpec((None, 1, d), mod), pl.BlockSpec((None, 1, d), mod), full(w),
                pl.BlockSpec((tm, 1), row)] + [full(a) for a in mla_args[1:]]
    in_specs += [full(a) for a in rwkv_args]
    args = [x, x, ln_g, ln_b, sc, sh, w, *mla_args, *rwkv_args]
    if vres is not None:
        in_specs += [pl.BlockSpec((tm, c), lambda i: (i, RWKV_PACK.index("v")))] + [full(a) for a in vres[1:]]
        args += list(vres)
    sub = tq // tm
    out_specs = [pl.BlockSpec((tm, hw), row), pl.BlockSpec((tm, hw), row),
                 pl.BlockSpec((None, None, hw, tm),
                              lambda i: (i // per_b, (i % per_b) // sub, 0, (i % per_b) % sub))]
    out_shape = [jax.ShapeDtypeStruct((t, hw), BF16), jax.ShapeDtypeStruct((t, hw), BF16),
                 jax.ShapeDtypeStruct((batch, seq // tq, hw, tq), BF16)]
    n_pack = len(RWKV_PACK)
    out_specs += [pl.BlockSpec((tm, n_pack * c), row)] + [pl.BlockSpec((tm, c), row)] * 2
    out_shape += [jax.ShapeDtypeStruct((t, n_pack * c), F32)] + [jax.ShapeDtypeStruct((t, c), F32)] * 2
    if do_ln:
        out_specs = [pl.BlockSpec((tm, d), row)] + out_specs
        out_shape = [jax.ShapeDtypeStruct((t, d), F32)] + out_shape
    return pl.pallas_call(
        functools.partial(_mix_prep_kernel, do_ln=do_ln, has_vres=vres is not None, per_b=per_b),
        grid=(t // tm,),
        in_specs=in_specs,
        out_specs=out_specs,
        out_shape=out_shape,
        compiler_params=_cparams(("parallel",)),
        name="mix_prep",
    )(*args)


def _mla_operands(pm, pos_ref, qg_ref, kvg_ref, wq_ref, wk_ref, wvt_ref, invf_ref, sgn_ref,
                  q_ref, k_ref, vt_ref):
    ang = pos_ref[...].astype(F32) * invf_ref[...]
    cos = jnp.cos(ang)
    sin = jnp.sin(ang) * sgn_ref[...]
    qn = _rms_norm(pm[:, :Q_LORA], qg_ref[...])
    q2 = jnp.dot(qn.astype(BF16), wq_ref[...], preferred_element_type=F32)
    kvn = _rms_norm(pm[:, Q_LORA:Q_LORA + KV_LORA], kvg_ref[...])
    kv = jnp.dot(kvn.astype(BF16), wk_ref[...], preferred_element_type=F32)
    v_t = jnp.dot(wvt_ref[...], kvn.T.astype(BF16), preferred_element_type=F32)
    ones_hi = (lax.broadcasted_iota(I32, (v_t.shape[0], 1), 0) % LANES >= V_HEAD).astype(F32)
    vt_ref[...] = (v_t + ones_hi).astype(BF16)
    off = Q_LORA + KV_LORA
    k_rot = pm[:, off:off + LANES] * cos + pm[:, off + LANES:off + 2 * LANES] * sin
    scale = QK_HEAD ** -0.5 * LOG2_E
    hw = MLA_HEADS * LANES
    for h in range(MLA_HEADS):
        sl = slice(h * LANES, (h + 1) * LANES)
        sl2 = slice(hw + h * LANES, hw + (h + 1) * LANES)
        q_ref[:, sl] = ((q2[:, sl] * cos + q2[:, sl2] * sin) * scale).astype(BF16)
        k_ref[:, sl] = (kv[:, sl] + k_rot).astype(BF16)


def _attn_kernel(q_ref, k_ref, vt_ref, o_ref, *, tq):
    qi = pl.program_id(2)
    key = lax.broadcasted_iota(I32, (tq, tq), 0)
    qry = lax.broadcasted_iota(I32, (tq, tq), 1)
    causal = key <= qry
    heads = range(ATTN_GROUP)
    qs = [q_ref[:, h * LANES:(h + 1) * LANES] for h in heads]

    def block(j, carry, masked):
        ms, accs = carry
        start = pl.multiple_of(j * tq, tq)
        ks = [k_ref[pl.ds(start, tq), h * LANES:(h + 1) * LANES] for h in heads]
        vts = [vt_ref[j, h * LANES:(h + 1) * LANES, :] for h in heads]
        ss = [lax.dot_general(ks[h], qs[h], (((1,), (1,)), ((), ())), preferred_element_type=F32)
              for h in heads]
        if masked:
            ss = [jnp.where(causal, s, -jnp.inf) for s in ss]
        m_new = [jnp.maximum(ms[h], jnp.max(ss[h], axis=0, keepdims=True)) for h in heads]
        alpha = [jnp.exp2(ms[h] - m_new[h]) for h in heads]
        ps = [jnp.exp2(ss[h] - m_new[h]).astype(BF16) for h in heads]
        acc_new = [alpha[h] * accs[h] + jnp.dot(vts[h], ps[h], preferred_element_type=F32) for h in heads]
        return tuple(m_new), tuple(acc_new)

    init = (tuple(jnp.full((1, tq), -jnp.inf, F32) for _ in heads),
            tuple(jnp.zeros((LANES, tq), F32) for _ in heads))
    carry = lax.fori_loop(0, qi, functools.partial(block, masked=False), init)
    _, accs = block(qi, carry, True)
    outs = [a[:V_HEAD] / a[V_HEAD:V_HEAD + 1] for a in accs]
    for p in range(ATTN_GROUP // 2):
        pair = jnp.concatenate([outs[2 * p], outs[2 * p + 1]], axis=0)
        o_ref[:, p * LANES:(p + 1) * LANES] = pair.T


def _attention(q, k, vt, batch, seq, tq):
    t = q.shape[0]
    nq = seq // tq
    gw = ATTN_GROUP * LANES
    return pl.pallas_call(
        functools.partial(_attn_kernel, tq=tq),
        grid=(batch, MLA_HEADS // ATTN_GROUP, nq),
        in_specs=[pl.BlockSpec((tq, gw), lambda b, h, i: (b * nq + i, h)),
                  pl.BlockSpec((seq, gw), lambda b, h, i: (b, h)),
                  pl.BlockSpec((None, nq, gw, tq), lambda b, h, i: (b, 0, h, 0))],
        out_specs=pl.BlockSpec((tq, gw // 2), lambda b, h, i: (b * nq + i, h)),
        out_shape=jax.ShapeDtypeStruct((t, MLA_OUT), F32),
        compiler_params=_cparams(("parallel", "parallel", "arbitrary")),
        name="attention",
    )(q, k, vt)


def _head_ones(n):
    r = lax.broadcasted_iota(I32, (n, n), 0) // RWKV_N
    c = lax.broadcasted_iota(I32, (n, n), 1) // RWKV_N
    return (r == c).astype(BF16)


def _head_sums(x):
    gw = RWKV_GROUP * RWKV_N
    ones = _head_ones(gw)
    return jnp.concatenate([_split_dot(x[:, g * gw:(g + 1) * gw], ones) for g in range(x.shape[1] // gw)],
                           axis=1)


def _rwkv_operands(p, prev_row, in_refs, out_refs, has_vres):
    if has_vres:
        (mu_ref, w0_ref, w2_ref, a0_ref, a2_ref, g2_ref, kk_ref, ka_ref, rk_ref,
         vf_ref, v0_ref, v1_ref, v2_ref) = in_refs
    else:
        mu_ref, w0_ref, w2_ref, a0_ref, a2_ref, g2_ref, kk_ref, ka_ref, rk_ref = in_refs
    pack_out, g_out, bonus_out = out_refs
    row = lax.broadcasted_iota(I32, p.shape, 0)
    p_prev = jnp.where(row == 0, prev_row, pltpu.roll(p, 1, 0))
    p = p + mu_ref[...] * (p_prev - p)
    c = RWKV_DIM
    r = p[:, :c]
    k = p[:, c:2 * c]
    v = p[:, 2 * c:3 * c]
    wd = p[:, 3 * c:3 * c + DECAY_LORA]
    ad = p[:, 3 * c + DECAY_LORA:3 * c + DECAY_LORA + ICLR_LORA]
    gd = p[:, 3 * c + DECAY_LORA + ICLR_LORA:]
    z = w0_ref[...] + _bdot(jnp.tanh(wd), w2_ref[...])
    y = -z
    softplus = jnp.maximum(y, 0.0) + jnp.log(1.0 + jnp.exp(-jnp.abs(y)))
    lw = -jnp.exp(-softplus - 0.5)
    a = jax.nn.sigmoid(a0_ref[...] + _bdot(ad, a2_ref[...]))
    g_out[...] = _bdot(jax.nn.sigmoid(gd), g2_ref[...])
    if has_vres:
        mix = jax.nn.sigmoid(v0_ref[...] + _bdot(_bdot(v, v1_ref[...]), v2_ref[...]))
        v = v + (vf_ref[...] - v) * mix
    kk = k * kk_ref[...]
    norm = jnp.sqrt(_head_sums(kk * kk))
    kk = kk / jnp.maximum(norm, 1e-12)
    k = k * (1.0 + (a - 1.0) * ka_ref[...])
    for j, val in enumerate((r, lw, k, v, -kk, kk * a)):
        pack_out[:, j * c:(j + 1) * c] = val
    bonus_out[...] = _head_sums(r * k * rk_ref[...]) * v


def _rwkv_rec_kernel(pk_ref, o_ref, st_ref, *, nb):
    cs = CHUNK
    gw = RWKV_GROUP * RWKV_N
    ng = RWKV_DIM // gw

    @pl.when(pl.program_id(1) == 0)
    def _():
        st_ref[...] = jnp.zeros_like(st_ref)

    row = lax.broadcasted_iota(I32, (cs, gw), 0)
    lane = lax.broadcasted_iota(I32, (cs, gw), 1)
    col = lane % cs
    head = lane // RWKV_N
    strict = col < row
    incl = col <= row
    eye_g = (col == row).astype(F32)
    r2 = lax.broadcasted_iota(I32, (gw, gw), 0)
    l2 = lax.broadcasted_iota(I32, (gw, gw), 1)
    same_head = (r2 // RWKV_N) == (l2 // RWKV_N)
    eye_sq = r2 == l2
    tri = (lax.broadcasted_iota(I32, (cs, cs), 1) <= lax.broadcasted_iota(I32, (cs, cs), 0)).astype(BF16)

    def bd(x):
        return jnp.concatenate([jnp.where(head == h, x, 0.0) for h in range(RWKV_GROUP)],
                               axis=0).astype(BF16)

    streams = [(b, g) for b in range(nb) for g in range(ng)]
    ld = lambda name: [pk_ref[b, :, pl.ds(RWKV_PACK.index(name) * RWKV_DIM + g * gw, gw)] for b, g in streams]
    r, lw, k, v, al, be = (ld(name) for name in RWKV_PACK)
    each = lambda f, *ls: [f(*xs) for xs in zip(*ls)]

    cum = each(lambda x: _split_dot_left(tri, x), lw)
    cum_end = each(lambda c_: c_[cs - 1:cs, :], cum)
    e_neg = each(lambda c_: jnp.exp(-c_), cum)
    e_end = each(lambda c_, ce: jnp.exp(ce - c_), cum, cum_end)
    a_bar = each(lambda a_, c_, w_: a_ * jnp.exp(c_ - w_), al, cum, lw)
    r_bar = each(lambda r_, c_: r_ * jnp.exp(c_), r, cum)
    b_bar = each(lambda x, e: x * e, be, e_neg)
    k_bar = each(lambda x, e: x * e, k, e_neg)
    b_til = each(lambda x, e: x * e, be, e_end)
    k_til = each(lambda x, e: x * e, k, e_end)
    gram = each(lambda a_, r_, b_, k_: _bdot_nt(jnp.concatenate([a_, r_], axis=0),
                                                jnp.concatenate([bd(b_), bd(k_)], axis=0)),
                a_bar, r_bar, b_bar, k_bar)
    l_ab = each(lambda g_: jnp.where(strict, g_[:cs, :gw], 0.0), gram)
    l_ak = each(lambda g_: jnp.where(strict, g_[:cs, gw:], 0.0), gram)
    m_rb = each(lambda g_: jnp.where(incl, g_[cs:, :gw], 0.0), gram)
    m_rk = each(lambda g_: jnp.where(incl, g_[cs:, gw:], 0.0), gram)
    x = each(lambda l_: eye_g + l_, l_ab)
    lp = each(lambda l_: _bdot(l_, bd(l_)), l_ab)
    for _ in range(4):
        xp = each(lambda x_, l_: _bdot(jnp.concatenate([x_, l_], axis=0), bd(l_)), x, lp)
        x = each(lambda x_, t_: x_ + t_[:cs], x, xp)
        lp = each(lambda t_: t_[cs:], xp)
    x = each(lambda x_, l_: x_ + _bdot(x_, bd(l_)), x, lp)
    lmv = each(lambda l_, m_, v_: _bdot(jnp.concatenate([l_, m_], axis=0), bd(v_)), l_ak, m_rk, v)
    lv = each(lambda t_: t_[:cs], lmv)
    mv = each(lambda t_: t_[cs:], lmv)
    w1 = each(lambda x_, a_: _bdot(x_, bd(a_)), x, a_bar)
    u0 = each(lambda x_, l_: _bdot(x_, bd(l_)), x, lv)
    bk_t = each(lambda b_, k_: jnp.concatenate([b_, k_], axis=0).T.astype(BF16), b_til, k_til)
    pc_col = each(lambda ce: jnp.sum(jnp.where(eye_sq, jnp.exp(ce), 0.0), axis=1, keepdims=True), cum_end)
    sb = [st_ref[s] for s in range(len(streams))]
    ws = each(lambda w_, r_, s_: _bdot(jnp.concatenate([w_, r_], axis=0), s_), w1, r_bar, sb)
    u = each(lambda w_, u_: w_[:cs] + u_, ws, u0)
    out = each(lambda w_, m_, u_, mv_: w_[cs:] + _bdot(m_, bd(u_)) + mv_, ws, m_rb, u, mv)
    upd = each(lambda t_, u_, v_: _bdot(t_, jnp.concatenate([u_, v_], axis=0)), bk_t, u, v)
    for s, (b, g) in enumerate(streams):
        o_ref[b, :, g * gw:(g + 1) * gw] = out[s]
        st_ref[s] = pc_col[s] * sb[s] + jnp.where(same_head, upd[s], 0.0)


def _split_dot_left(ones_bf16, x):
    hi = x.astype(BF16)
    lo = (x - hi.astype(F32)).astype(BF16)
    return (jnp.dot(ones_bf16, hi, preferred_element_type=F32)
            + jnp.dot(ones_bf16, lo, preferred_element_type=F32))


def _rwkv_recurrence(packed, batch, seq, nb):
    t, wide = packed.shape
    c = RWKV_DIM
    nc = seq // CHUNK
    gw = RWKV_GROUP * RWKV_N
    out = pl.pallas_call(
        functools.partial(_rwkv_rec_kernel, nb=nb),
        grid=(batch // nb, nc),
        in_specs=[pl.BlockSpec((nb, CHUNK, wide), lambda b, j: (b, j, 0))],
        out_specs=pl.BlockSpec((nb, CHUNK, c), lambda b, j: (b, j, 0)),
        out_shape=jax.ShapeDtypeStruct((batch, seq, c), F32),
        scratch_shapes=[pltpu.VMEM((nb * (c // gw), gw, gw), F32)],
        compiler_params=_cparams(("parallel", "arbitrary")),
        name="rwkv_recurrence",
    )(packed.reshape(batch, seq, wide))
    return out.reshape(t, c)


def _outproj_kernel(x_ref, att_ref, rec_ref, bonus_ref, g_ref, og_ref, gng_ref, gnb_ref, wo_ref,
                    gt_ref, lng_ref, lnb_ref, sc_ref, sh_ref, rwh_ref, rwl_ref, rb_ref,
                    x1_ref, h2_ref, pg_ref, pgt_ref, meta_ref, run_ref):
    step = pl.program_id(0)

    @pl.when(step == 0)
    def _():
        run_ref[...] = jnp.zeros_like(run_ref)

    tm = x_ref.shape[0]
    mla = _rms_norm(att_ref[...], og_ref[...])
    o = rec_ref[...]
    mean = _head_sums(o) * (1.0 / RWKV_N)
    oc = o - mean
    var = _head_sums(oc * oc) * (1.0 / RWKV_N)
    rw = (oc * lax.rsqrt(var + RWKV_GN_EPS) * gng_ref[...] + gnb_ref[...] + bonus_ref[...]) * g_ref[...]
    mix = (jnp.dot(mla.astype(BF16), wo_ref[:MLA_OUT, :], preferred_element_type=F32)
           + jnp.dot(rw.astype(BF16), wo_ref[MLA_OUT:, :], preferred_element_type=F32))
    x1 = _layer_norm(DEEPNORM_ALPHA * x_ref[...] + (1.0 + gt_ref[...]) * mix, lng_ref[...], lnb_ref[...])
    x1_ref[...] = x1
    h2 = x1 * (1.0 + sc_ref[...]) + sh_ref[...]
    h2_ref[...] = h2.astype(BF16)
    h_hi = h2.astype(BF16)
    h_lo = (h2 - h_hi.astype(F32)).astype(BF16)
    logits = (_bdot_nt(rwh_ref[...], h_hi) + _bdot_nt(rwh_ref[...], h_lo) + _bdot_nt(rwl_ref[...], h_hi)
              + rb_ref[...])
    ne = N_EXPERTS
    e_iota = lax.broadcasted_iota(I32, (ne, tm), 0)
    work = logits
    vals, hots = [], []
    for kk in range(TOP_K):
        m = jnp.max(work, axis=0, keepdims=True)
        sel = jnp.min(jnp.where(work == m, e_iota, ne), axis=0, keepdims=True)
        hot = e_iota == sel
        vals.append(m)
        hots.append(hot)
        work = jnp.where(hot, -jnp.inf, work)
    exps = [jnp.exp(vv - vals[0]) for vv in vals]
    denom = exps[0] + exps[1] + exps[2] + exps[3]
    any_hot = (hots[0] | hots[1] | hots[2] | hots[3]).astype(BF16)
    earlier = (lax.broadcasted_iota(I32, (tm, tm), 0) < lax.broadcasted_iota(I32, (tm, tm), 1)).astype(BF16)
    before = jnp.dot(any_hot, earlier, preferred_element_type=F32)
    cnt_col = jnp.sum(any_hot.astype(F32), axis=1, keepdims=True)
    lower = (lax.broadcasted_iota(I32, (ne, ne), 1) < lax.broadcasted_iota(I32, (ne, ne), 0)).astype(BF16)
    excl_col = _split_dot_left(lower, jnp.broadcast_to(cnt_col, (ne, LANES)))[:, 0:1]
    base = before + excl_col
    rows = [jnp.sum(jnp.where(hots[kk], base, 0.0), axis=0, keepdims=True) for kk in range(TOP_K)]
    rows += [exps[kk] / denom for kk in range(TOP_K)]
    pg_t = jnp.concatenate(rows, axis=0)
    pgt_ref[...] = pg_t
    pad = jnp.zeros((LANES - 2 * TOP_K, tm), F32)
    pg_ref[...] = jnp.concatenate([pg_t, pad], axis=0).T
    eye = lax.broadcasted_iota(I32, (ne, ne), 0) == lax.broadcasted_iota(I32, (ne, ne), 1)
    as_row = lambda col: jnp.sum(jnp.where(eye, col, 0.0), axis=0, keepdims=True)
    tile_cnt = as_row(cnt_col)
    mrow = lax.broadcasted_iota(I32, (SUBLANES, ne), 0)
    meta = jnp.where(mrow == 0, tile_cnt,
                     jnp.where(mrow == 1, as_row(excl_col), jnp.where(mrow == 2, run_ref[...], 0.0)))
    meta_ref[...] = meta.astype(I32)
    run_ref[...] = run_ref[...] + tile_cnt


def _outproj(x, att, rec, bonus, g, out_g, gn_g, gn_b, wo, gt, ln_g, ln_b, sc, sh, rw, rb, seq, tm):
    t, d = x.shape
    rw_t = rw.T
    rw_hi = rw_t.astype(BF16)
    rw_lo = (rw_t - rw_hi.astype(F32)).astype(BF16)
    rb = rb.reshape(N_EXPERTS, 1)
    per_b = seq // tm
    row = lambda i: (i, 0)
    const = lambda i: (0, 0)
    mod = lambda i: (i // per_b, 0, 0)
    vec = lambda n: pl.BlockSpec((1, n), const)
    half = lambda: pl.BlockSpec((tm, RWKV_DIM), row)
    n_tiles = t // tm
    return pl.pallas_call(
        _outproj_kernel,
        grid=(t // tm,),
        in_specs=[pl.BlockSpec((tm, d), row), half(), half(), half(), half(),
                  vec(MLA_OUT), vec(RWKV_DIM), vec(RWKV_DIM), pl.BlockSpec(wo.shape, const),
                  pl.BlockSpec((None, 1, d), mod), vec(d), vec(d),
                  pl.BlockSpec((None, 1, d), mod), pl.BlockSpec((None, 1, d), mod),
                  pl.BlockSpec(rw_hi.shape, const), pl.BlockSpec(rw_lo.shape, const),
                  pl.BlockSpec((N_EXPERTS, 1), const)],
        out_specs=[pl.BlockSpec((tm, d), row), pl.BlockSpec((tm, d), row),
                   pl.BlockSpec((tm, LANES), row),
                   pl.BlockSpec((None, 2 * TOP_K, tm), lambda i: (i, 0, 0)),
                   pl.BlockSpec((None, SUBLANES, N_EXPERTS), lambda i: (i, 0, 0))],
        out_shape=[jax.ShapeDtypeStruct((t, d), F32), jax.ShapeDtypeStruct((t, d), BF16),
                   jax.ShapeDtypeStruct((t, LANES), F32),
                   jax.ShapeDtypeStruct((n_tiles, 2 * TOP_K, tm), F32),
                   jax.ShapeDtypeStruct((n_tiles, SUBLANES, N_EXPERTS), I32)],
        scratch_shapes=[pltpu.VMEM((1, N_EXPERTS), F32)],
        compiler_params=_cparams(("arbitrary",)),
        name="outproj_router",
    )(x, att, rec, bonus, g, out_g, gn_g, gn_b, wo, gt, ln_g, ln_b, sc, sh, rw_hi, rw_lo, rb)


def _segment_copies(count, src_row, dst_row, src_ref, dst_ref, sem, max_bits, act):
    def bits(lo, hi):
        for b in range(lo, hi):
            size = 1 << b

            @pl.when((count & size) != 0)
            def _():
                off = count & (size - 1)
                s = pl.multiple_of((src_row + off) * SUBLANES, SUBLANES)
                d = pl.multiple_of((dst_row + off) * SUBLANES, SUBLANES)
                act(pltpu.make_async_copy(src_ref.at[pl.ds(s, size * SUBLANES)],
                                          dst_ref.at[pl.ds(d, size * SUBLANES)], sem), b % 2)

    split = min(max_bits, SEG_COMMON_BITS)
    bits(0, split)
    if max_bits > split:
        @pl.when(count >= (1 << split))
        def _():
            bits(split, max_bits)


def _start(copy, priority=0):
    copy.start(priority=priority)


def _wait(copy, priority=0):
    del priority
    copy.wait()


def _dispatch_kernel(cnt_ref, src_ref, dst_ref, zcnt_ref, zdst_ref, nused_ref, h_ref, pos_ref, xs_hbm,
                     sorted_ref, zeros_ref, sems, zsem, *, ts, bm):
    tau = pl.program_id(0)
    n_tiles = pl.num_programs(0)
    slot = tau % 2
    n_rows = TOP_K * ts
    seg_bits = n_rows.bit_length()
    pad_bits = (bm - 1).bit_length()

    def drain(sl):
        pltpu.make_async_copy(sorted_ref.at[sl], xs_hbm.at[pl.ds(0, n_rows * SUBLANES)], sems.at[sl]).wait()

    @pl.when(tau >= 2)
    def _():
        drain(slot)

    pos_t = pos_ref[...]
    a = lax.broadcasted_iota(I32, (n_rows, ts), 0).astype(F32)
    perm = (pos_t[0:1] == a) | (pos_t[1:2] == a) | (pos_t[2:3] == a) | (pos_t[3:4] == a)
    srt = jnp.dot(perm.astype(BF16), h_ref[...], preferred_element_type=F32)
    for j in range(D_MODEL // LANES):
        sorted_ref[slot, pl.ds(j, n_rows, stride=SUBLANES), :] = srt[:, j * LANES:(j + 1) * LANES]

    def issue(e, carry):
        i = tau * N_EXPERTS + e
        _segment_copies(cnt_ref[i], src_ref[i], dst_ref[i], sorted_ref.at[slot], xs_hbm, sems.at[slot],
                        seg_bits, _start)
        return carry

    lax.fori_loop(0, N_EXPERTS, issue, 0)

    @pl.when(tau == 0)
    def _():
        zeros_ref[...] = jnp.zeros_like(zeros_ref)

        def pad(act):
            def body(e, carry):
                _segment_copies(zcnt_ref[e], 0, zdst_ref[e], zeros_ref, xs_hbm, zsem, pad_bits, act)
                return carry
            lax.fori_loop(0, N_EXPERTS, body, 0)

        def tail(act):
            def body(blk, carry):
                row0 = pl.multiple_of(blk * (bm * SUBLANES), bm * SUBLANES)
                act(pltpu.make_async_copy(zeros_ref, xs_hbm.at[pl.ds(row0, bm * SUBLANES)], zsem))
                return carry
            lax.fori_loop(nused_ref[0], xs_hbm.shape[0] // (bm * SUBLANES), body, 0)

        pad(_start)
        tail(_start)
        pad(_wait)
        tail(_wait)

    @pl.when(tau == n_tiles - 1)
    def _():
        drain(slot)

    @pl.when((tau == n_tiles - 1) & (tau >= 1))
    def _():
        drain(1 - slot)


def _dispatch(seg_cnt, seg_src, seg_dst, pad_cnt, pad_dst, n_used, h2, pos, n_slots, ts, bm):
    t, d = h2.shape
    n_rows = TOP_K * ts
    grid_spec = pltpu.PrefetchScalarGridSpec(
        num_scalar_prefetch=6,
        grid=(t // ts,),
        in_specs=[pl.BlockSpec((ts, d), lambda i, *_: (i, 0)),
                  pl.BlockSpec((None, 2 * TOP_K, ts), lambda i, *_: (i, 0, 0))],
        out_specs=pl.BlockSpec(memory_space=pl.ANY),
        scratch_shapes=[pltpu.VMEM((2, n_rows * SUBLANES, LANES), F32),
                        pltpu.VMEM((bm * SUBLANES, LANES), F32),
                        pltpu.SemaphoreType.DMA((2,)), pltpu.SemaphoreType.DMA(())],
    )
    return pl.pallas_call(
        functools.partial(_dispatch_kernel, ts=ts, bm=bm),
        grid_spec=grid_spec,
        out_shape=jax.ShapeDtypeStruct((n_slots * SUBLANES, LANES), F32),
        compiler_params=_cparams(("arbitrary",)),
        name="dispatch",
    )(seg_cnt, seg_src, seg_dst, pad_cnt, pad_dst, n_used, h2, pos)


def _ffn_kernel(be_ref, nused_ref, nxt_ref, par_ref, x_ref, w1_hbm, b1_ref, w2_hbm, b2_ref, y_ref,
                w1f_ref, w2f_ref, w1s_ref, w2s_ref, sems, *, bm, layer):
    i = pl.program_id(0)
    f = w2s_ref.shape[0]
    used = i < nused_ref[0]
    new_expert = (i == 0) | (be_ref[i] != be_ref[jnp.maximum(i - 1, 0)])

    def fetch(e, slot, act):
        act(pltpu.make_async_copy(w1_hbm.at[layer, e], w1f_ref.at[slot], sems.at[0, slot]))
        act(pltpu.make_async_copy(w2_hbm.at[layer, e], w2f_ref.at[slot], sems.at[1, slot]))

    @pl.when(used & new_expert)
    def _():
        slot = par_ref[i]

        @pl.when(i == 0)
        def _():
            fetch(be_ref[0], slot, _start)

        fetch(be_ref[i], slot, _wait)

        @pl.when(nxt_ref[i] >= 0)
        def _():
            fetch(nxt_ref[i], 1 - slot, _start)

        grp = 2 * LANES
        c = lax.broadcasted_iota(I32, (grp, grp), 0)
        j = lax.broadcasted_iota(I32, (grp, grp), 1)
        perm = (c == jnp.where(j < LANES, 2 * j, 2 * (j - LANES) + 1)).astype(BF16)
        for g in range(w1s_ref.shape[1] // grp):
            y = jnp.dot(w1f_ref[slot, :, g * grp:(g + 1) * grp].astype(BF16), perm,
                        preferred_element_type=F32)
            w1s_ref[:, g * LANES:(g + 1) * LANES] = y[:, :LANES].astype(BF16)
            w1s_ref[:, f + g * LANES:f + (g + 1) * LANES] = y[:, LANES:].astype(BF16)
        w2s_ref[...] = w2f_ref[slot].astype(BF16)

    @pl.when(used)
    def _():
        x = jnp.concatenate([x_ref[pl.ds(j, bm, stride=SUBLANES), :] for j in range(D_MODEL // LANES)],
                            axis=1)
        hh = jnp.dot(x.astype(BF16), w1s_ref[...], preferred_element_type=F32) + b1_ref[...]
        x_glu = jnp.minimum(hh[:, :f], SWIGLU_LIMIT)
        x_lin = jnp.clip(hh[:, f:], -SWIGLU_LIMIT, SWIGLU_LIMIT)
        u = x_glu * jax.nn.sigmoid(SWIGLU_ALPHA * x_glu) * (x_lin + 1.0)
        y = jnp.dot(u.astype(BF16), w2s_ref[...], preferred_element_type=F32) + b2_ref[...]
        for j in range(D_MODEL // LANES):
            y_ref[pl.ds(j, bm, stride=SUBLANES), :] = y[:, j * LANES:(j + 1) * LANES]

    @pl.when(i >= nused_ref[0])
    def _():
        y_ref[...] = jnp.zeros_like(y_ref)


def _expert_ffn(block_expert, n_used, next_expert, parity, xs, w1, b1, w2, b2, layer, bm):
    n_blocks = block_expert.shape[0]
    d = w1.shape[2]
    f = w2.shape[2]
    last_used = lambda i, nu: jnp.maximum(jnp.minimum(i, nu[0] - 1), 0)
    of_expert = lambda i, be, *_: (layer, be[i], 0, 0)
    grid_spec = pltpu.PrefetchScalarGridSpec(
        num_scalar_prefetch=4,
        grid=(n_blocks,),
        in_specs=[pl.BlockSpec((bm * SUBLANES, LANES), lambda i, be, nu, *_: (last_used(i, nu), 0)),
                  pl.BlockSpec(memory_space=pl.ANY),
                  pl.BlockSpec((None, None, 1, 2 * f), of_expert),
                  pl.BlockSpec(memory_space=pl.ANY),
                  pl.BlockSpec((None, None, 1, d), of_expert)],
        out_specs=pl.BlockSpec((bm * SUBLANES, LANES), lambda i, *_: (i, 0)),
        scratch_shapes=[pltpu.VMEM((2, d, 2 * f), F32), pltpu.VMEM((2, f, d), F32),
                        pltpu.VMEM((d, 2 * f), BF16), pltpu.VMEM((f, d), BF16),
                        pltpu.SemaphoreType.DMA((2, 2))],
    )
    return pl.pallas_call(
        functools.partial(_ffn_kernel, bm=bm, layer=layer),
        grid_spec=grid_spec,
        out_shape=jax.ShapeDtypeStruct(xs.shape, F32),
        compiler_params=_cparams(("arbitrary",)),
        name="expert_ffn",
    )(block_expert, n_used, next_expert, parity, xs, w1, b1, w2, b2)


def _combine_kernel(cnt_ref, src_ref, dst_ref, y_hbm, x_ref, pg_ref, gt_ref, lng_ref, lnb_ref,
                    o_ref, ybuf, sems, *, ts):
    tau = pl.program_id(0)
    n_tiles = pl.num_programs(0)
    slot = tau % 2
    n_rows = TOP_K * ts
    seg_bits = n_rows.bit_length()

    def fetch(tile, sl):
        def body(e, carry):
            i = tile * N_EXPERTS + e
            _segment_copies(cnt_ref[i], dst_ref[i], src_ref[i], y_hbm, ybuf.at[sl], sems.at[sl],
                            seg_bits, _start)
            return carry
        lax.fori_loop(0, N_EXPERTS, body, 0)

    @pl.when(tau == 0)
    def _():
        fetch(0, 0)

    @pl.when(tau + 1 < n_tiles)
    def _():
        fetch(tau + 1, 1 - slot)

    pltpu.make_async_copy(y_hbm.at[pl.ds(0, n_rows * SUBLANES)], ybuf.at[slot], sems.at[slot]).wait()
    y = jnp.concatenate([ybuf[slot, pl.ds(j, n_rows, stride=SUBLANES), :] for j in range(D_MODEL // LANES)],
                        axis=1)
    pg = pg_ref[...]
    a = lax.broadcasted_iota(I32, (ts, n_rows), 1).astype(F32)
    gmat = jnp.zeros((ts, n_rows), F32)
    for kk in range(TOP_K):
        gmat = gmat + jnp.where(pg[:, kk:kk + 1] == a, pg[:, TOP_K + kk:TOP_K + kk + 1], 0.0)
    g_hi = gmat.astype(BF16)
    g_lo = (gmat - g_hi.astype(F32)).astype(BF16)
    y_hi = y.astype(BF16)
    y_lo = (y - y_hi.astype(F32)).astype(BF16)
    ffn = (jnp.dot(g_hi, y_hi, preferred_element_type=F32) + jnp.dot(g_hi, y_lo, preferred_element_type=F32)
           + jnp.dot(g_lo, y_hi, preferred_element_type=F32))
    o_ref[...] = _layer_norm(DEEPNORM_ALPHA * x_ref[...] + (1.0 + gt_ref[...]) * ffn,
                             lng_ref[...], lnb_ref[...])


def _combine(seg_cnt, seg_src, seg_dst, y, x1, pg, gt, ln_g, ln_b, seq, ts):
    t, d = x1.shape
    per_b = seq // ts
    n_rows = TOP_K * ts
    row = lambda i, *_: (i, 0)
    const = lambda i, *_: (0, 0)
    grid_spec = pltpu.PrefetchScalarGridSpec(
        num_scalar_prefetch=3,
        grid=(t // ts,),
        in_specs=[pl.BlockSpec(memory_space=pl.ANY),
                  pl.BlockSpec((ts, d), row), pl.BlockSpec((ts, LANES), row),
                  pl.BlockSpec((None, 1, d), lambda i, *_: (i // per_b, 0, 0)),
                  pl.BlockSpec((1, d), const), pl.BlockSpec((1, d), const)],
        out_specs=pl.BlockSpec((ts, d), row),
        scratch_shapes=[pltpu.VMEM((2, n_rows * SUBLANES, LANES), F32), pltpu.SemaphoreType.DMA((2,))],
    )
    return pl.pallas_call(
        functools.partial(_combine_kernel, ts=ts),
        grid_spec=grid_spec,
        out_shape=jax.ShapeDtypeStruct((t, d), F32),
        compiler_params=_cparams(("arbitrary",)),
        name="combine",
    )(seg_cnt, seg_src, seg_dst, y, x1, pg, gt, ln_g, ln_b)


def _rope_cols():
    half = QK_ROPE // 2
    return np.arange(half) * 2, np.arange(half) * 2 + 1


def _build_w_in(w_in):
    d = w_in.shape[0]
    even, odd = _rope_cols()
    kr = w_in[:, Q_LORA + KV_LORA:MLA_IN]
    zeros = lambda n: jnp.zeros((d, n), w_in.dtype)
    placed = jnp.concatenate([zeros(QK_NOPE), kr[:, even], kr[:, odd], zeros(LANES - QK_HEAD)], axis=1)
    swapped = jnp.concatenate([zeros(QK_NOPE), kr[:, odd], kr[:, even], zeros(LANES - QK_HEAD)], axis=1)
    return jnp.concatenate([w_in[:, :Q_LORA + KV_LORA], placed, swapped, w_in[:, MLA_IN:]],
                           axis=1).astype(BF16)


def _build_w_uq(w_uq):
    even, odd = _rope_cols()
    w = w_uq.reshape(Q_LORA, MLA_HEADS, QK_HEAD)
    nope = w[:, :, :QK_NOPE]
    rot = w[:, :, QK_NOPE:]
    pad = jnp.zeros((Q_LORA, MLA_HEADS, LANES - QK_HEAD), w_uq.dtype)
    zero_nope = jnp.zeros_like(nope)
    placed = jnp.concatenate([nope, rot[:, :, even], rot[:, :, odd], pad], axis=2)
    swapped = jnp.concatenate([zero_nope, rot[:, :, odd], rot[:, :, even], pad], axis=2)
    return jnp.concatenate([placed.reshape(Q_LORA, -1), swapped.reshape(Q_LORA, -1)], axis=1).astype(BF16)


def _build_w_ukv(w_uk, w_uv):
    w = w_uk.reshape(KV_LORA, MLA_HEADS, QK_NOPE)
    pad = jnp.zeros((KV_LORA, MLA_HEADS, LANES - QK_NOPE), w_uk.dtype)
    placed = jnp.concatenate([w, pad], axis=2).reshape(KV_LORA, -1)
    wv = w_uv.reshape(KV_LORA, MLA_HEADS, V_HEAD)
    vpad = jnp.zeros((KV_LORA, MLA_HEADS, LANES - V_HEAD), w_uv.dtype)
    v_placed = jnp.concatenate([wv, vpad], axis=2).reshape(KV_LORA, -1)
    return placed.astype(BF16), v_placed.T.astype(BF16)


def _rope_rows():
    half = QK_ROPE // 2
    inv = ROPE_THETA ** (-np.arange(0, QK_ROPE, 2, dtype=np.float32) / QK_ROPE)
    invf = np.zeros((1, LANES), np.float32)
    sgn = np.zeros((1, LANES), np.float32)
    invf[0, QK_NOPE:QK_NOPE + half] = inv
    invf[0, QK_NOPE + half:QK_HEAD] = inv
    sgn[0, QK_NOPE:QK_NOPE + half] = -1.0
    sgn[0, QK_NOPE + half:QK_HEAD] = 1.0
    return jnp.asarray(invf), jnp.asarray(sgn)


def kernel(x, c, positions, emb_ln_g, emb_ln_b, ada_w, ada_b, w_in, q_norm_g, w_uq, kv_norm_g, w_uk, w_uv, mla_out_g, rwkv_mu, rwkv_w0, rwkv_w2, rwkv_a0, rwkv_a2, rwkv_g2, rwkv_k_k, rwkv_k_a, rwkv_r_k, rwkv_gn_g, rwkv_gn_b, vres_v0, vres_v1, vres_v2, w_o, ln1_g, ln1_b, router_w, router_b, exp_w1, exp_b1, exp_w2, exp_b2, ln2_g, ln2_b):
    batch, seq, d = x.shape
    t = batch * seq
    tm, tq, ts, bm = _tile_sizes(seq)
    n_blocks = (t * TOP_K) // bm + N_EXPERTS
    n_slots = n_blocks * bm

    row = lambda a: a.reshape(1, -1)
    mod = _modulation(c, ada_w, ada_b)
    invf, sgn = _rope_rows()
    pos = positions.reshape(t, 1)
    xf = x.reshape(t, d)
    b1_all = jnp.concatenate([exp_b1[..., 0::2], exp_b1[..., 1::2]], axis=-1)[:, :, None, :]
    b2_all = exp_b2[:, :, None, :]
    v_first = None
    for i in range(DEPTH):
        sh1, sc1, gt1, sh2, sc2, gt2 = [m.reshape(batch, 1, d) for m in jnp.split(mod[i], 6, axis=-1)]
        wk, wvt = _build_w_ukv(w_uk[i], w_uv[i])
        mla_args = (pos, row(q_norm_g[i]), row(kv_norm_g[i]), _build_w_uq(w_uq[i]), wk, wvt, invf, sgn)
        rwkv_args = (row(rwkv_mu[i]), row(rwkv_w0[i]), rwkv_w2[i].astype(BF16), row(rwkv_a0[i]),
                     rwkv_a2[i].astype(BF16), rwkv_g2[i].astype(BF16), row(rwkv_k_k[i]),
                     row(rwkv_k_a[i]), row(rwkv_r_k[i]))
        vres = None if i == 0 else (v_first, row(vres_v0[i - 1]), vres_v1[i - 1].astype(BF16),
                                    vres_v2[i - 1].astype(BF16))
        outs = _mix_prep(xf, row(emb_ln_g), row(emb_ln_b), sc1, sh1, _build_w_in(w_in[i]), mla_args,
                         rwkv_args, vres, batch, seq, tm, tq, do_ln=(i == 0))
        if i == 0:
            xn, outs = outs[0], outs[1:]
        else:
            xn = xf
        q, k, vt, packed, g_, bonus = outs
        att = _attention(q, k, vt, batch, seq, tq)
        if i == 0:
            v_first = packed
        rec = _rwkv_recurrence(packed, batch, seq, nb=min(4, batch))
        x1, h2, pg, pg_t, meta = _outproj(
            xn, att, rec, bonus, g_, row(mla_out_g[i]), row(rwkv_gn_g[i]), row(rwkv_gn_b[i]),
            w_o[i].astype(BF16), gt1, row(ln1_g[i]), row(ln1_b[i]), sc2, sh2, router_w[i],
            row(router_b[i]), seq, ts)
        tile_cnt, tile_src, tile_before = meta[:, 0, :], meta[:, 1, :], meta[:, 2, :]
        cnt = tile_before[-1] + tile_cnt[-1]
        padded = (cnt + bm - 1) // bm * bm
        pad_ends = jnp.cumsum(padded)
        pad_starts = pad_ends - padded
        seg_cnt = tile_cnt.reshape(-1)
        seg_src = tile_src.reshape(-1)
        seg_dst = (pad_starts[None, :] + tile_before).reshape(-1)
        block_start = jnp.arange(n_blocks, dtype=I32) * bm
        block_expert = jnp.minimum(jnp.sum((pad_ends[None, :] <= block_start[:, None]).astype(I32), axis=1),
                                   N_EXPERTS - 1)
        n_used = (pad_ends[-1:] // bm).astype(I32)
        xs = _dispatch(seg_cnt, seg_src, seg_dst, padded - cnt, pad_starts + cnt, n_used, h2, pg_t,
                       n_slots, ts, bm)
        blk = jnp.arange(n_blocks, dtype=I32)
        first_of_expert = (blk == 0) | (block_expert != jnp.roll(block_expert, 1))
        parity = (jnp.cumsum(first_of_expert.astype(I32)) - 1) % 2
        following = pad_ends[block_expert] // bm
        next_expert = jnp.where(following < n_used[0],
                                block_expert[jnp.minimum(following, n_blocks - 1)], -1).astype(I32)
        y = _expert_ffn(block_expert, n_used, next_expert, parity.astype(I32), xs, exp_w1, b1_all, exp_w2,
                        b2_all, i, bm)
        xf = _combine(seg_cnt, seg_src, seg_dst, y, x1, pg, gt2, row(ln2_g[i]), row(ln2_b[i]), seq, ts)
    return xf.reshape(batch, seq, d)
```
